```python
import jax, jax.numpy as jnp
from jax import lax
import numpy as np

D_MODEL = 1024
BATCH = 2
SEQ = 8192
DEPTH = 2

N_MIXERS = 2
EPS = 1e-6

DSW_PATTERNS = ((128, 1), (512, 4), (2048, 16))
DSW_GROUPS = len(DSW_PATTERNS)
DSW_HEADS = 16
DSW_HEAD_DIM = D_MODEL // DSW_HEADS
DSW_IN_DIM = 3 * DSW_GROUPS * DSW_HEADS * DSW_HEAD_DIM

GLA_HEADS = 4
GLA_DK = D_MODEL // 2
GLA_DV = D_MODEL
GLA_HK = GLA_DK // GLA_HEADS
GLA_HV = GLA_DV // GLA_HEADS
GLA_GATE_RANK = 16
GLA_TAU = 16.0
GLA_CHUNK = 64
GLA_SPLITS = (GLA_DK, 2 * GLA_DK, 2 * GLA_DK + GLA_DV, 2 * GLA_DK + GLA_DV + GLA_GATE_RANK)
GLA_IN_DIM = 2 * GLA_DK + 2 * GLA_DV + GLA_GATE_RANK

FFN_DIM = 2816
CONV_WIDTH = 3

N_ATTN_LAYERS = (DEPTH + 1) // N_MIXERS
N_GLA_LAYERS = DEPTH // N_MIXERS

kernel_name = "hybrid_dilated_attn_gla_convffn_adaln"


def rms_norm(x, gain):
    xf = x.astype(jnp.float32)
    y = xf * lax.rsqrt(jnp.mean(xf * xf, axis=-1, keepdims=True) + EPS)
    return (y * gain.astype(jnp.float32)).astype(x.dtype)


def modulate(x, gain, shift, scale):
    return rms_norm(x, gain) * (1 + scale[:, None, :]) + shift[:, None, :]


def dilated_window_attention(q, k, v, window, dilation):
    B, S, H, E = q.shape
    n_keys = window // dilation
    L = S // dilation
    nb = -(-L // n_keys)
    L_pad = nb * n_keys

    def to_blocks(t):
        t = t.reshape(B, L, dilation, H, E).transpose(0, 2, 1, 3, 4)
        t = jnp.pad(t, ((0, 0), (0, 0), (0, L_pad - L), (0, 0), (0, 0)))
        return t.reshape(B, dilation, nb, n_keys, H, E)

    def with_prev(t):
        prev = jnp.pad(t, ((0, 0), (0, 0), (1, 0), (0, 0), (0, 0), (0, 0)))[:, :, :-1]
        return jnp.concatenate([prev, t], axis=3)

    qb = to_blocks(q)
    kb = with_prev(to_blocks(k))
    vb = with_prev(to_blocks(v))

    s = jnp.einsum('brnqhe,brnkhe->brnhqk', qb, kb,
                   preferred_element_type=jnp.float32) * (E ** -0.5)
    qi = jnp.arange(n_keys)[:, None] + n_keys
    kj = jnp.arange(2 * n_keys)[None, :]
    dist = qi - kj
    band = (dist >= 0) & (dist <= n_keys)
    not_first = (jnp.arange(nb) > 0)[:, None, None]
    valid = band[None] & (not_first | (kj >= n_keys)[None])
    s = jnp.where(valid[:, None], s, -jnp.inf)

    m = jnp.max(s, axis=-1, keepdims=True)
    p = jnp.exp(s - m)
    denom = jnp.sum(p, axis=-1, keepdims=True)
    o = jnp.einsum('brnhqk,brnkhe->brnqhe', p / denom, vb.astype(jnp.float32))
    lse = jnp.swapaxes((m + jnp.log(denom))[..., 0], 3, 4)

    def from_blocks(t):
        t = t.reshape((B, dilation, L_pad) + t.shape[4:])[:, :, :L]
        t = jnp.swapaxes(t, 1, 2)
        return t.reshape((B, S) + t.shape[3:])

    return from_blocks(o), from_blocks(lse)


def dilated_mixture_attention(h, w_in, w_out):
    B, S, _ = h.shape
    qkv = (h @ w_in).reshape(B, S, 3, DSW_GROUPS, DSW_HEADS, DSW_HEAD_DIM)
    outs, lses = [], []
    for g, (window, dilation) in enumerate(DSW_PATTERNS):
        o, lse = dilated_window_attention(qkv[:, :, 0, g], qkv[:, :, 1, g], qkv[:, :, 2, g],
                                          window, dilation)
        outs.append(o)
        lses.append(lse)
    weights = jax.nn.softmax(jnp.stack(lses, axis=0), axis=0)
    o = jnp.sum(weights[..., None] * jnp.stack(outs, axis=0), axis=0)
    return o.reshape(B, S, D_MODEL).astype(h.dtype) @ w_out


def gated_linear_attention(h, w_in, w_gate, b_gate, norm_gain, w_out):
    B, S, _ = h.shape
    f32 = jnp.float32
    q, k, v, g_low, r = jnp.split(h @ w_in, list(GLA_SPLITS), axis=-1)
    log_a = jax.nn.log_sigmoid((g_low @ w_gate + b_gate).astype(f32)) / GLA_TAU

    nc = S // GLA_CHUNK

    def chunks(t, hd):
        return t.astype(f32).reshape(B, nc, GLA_CHUNK, GLA_HEADS, hd)

    q = chunks(q, GLA_HK) * (GLA_HK ** -0.5)
    k = chunks(k, GLA_HK)
    v = chunks(v, GLA_HV)
    b = jnp.cumsum(chunks(log_a, GLA_HK), axis=2)
    b_last = b[:, :, -1:]
    q_dec = q * jnp.exp(b)
    k_dec = k * jnp.exp(-b)
    k_to_end = k * jnp.exp(b_last - b)

    causal = jnp.tril(jnp.ones((GLA_CHUNK, GLA_CHUNK), dtype=bool))
    att = jnp.where(causal, jnp.einsum('bnihk,bnjhk->bnhij', q_dec, k_dec), 0.0)
    o_intra = jnp.einsum('bnhij,bnjhv->bnihv', att, v)

    d_state = jnp.einsum('bnjhk,bnjhv->bnhkv', k_to_end, v)
    decay = jnp.exp(b_last[:, :, 0])

    def step(state, inp):
        dec, ds = inp
        return dec[..., None] * state + ds, state

    s0 = jnp.zeros((B, GLA_HEADS, GLA_HK, GLA_HV), f32)
    _, starts = lax.scan(step, s0, (jnp.moveaxis(decay, 1, 0), jnp.moveaxis(d_state, 1, 0)))
    starts = jnp.moveaxis(starts, 0, 1)
    o_inter = jnp.einsum('bnihk,bnhkv->bnihv', q_dec, starts)

    o = (o_intra + o_inter).reshape(B, S, GLA_HEADS, GLA_HV)
    o = rms_norm(o, norm_gain) * jax.nn.silu(r.astype(f32).reshape(B, S, GLA_HEADS, GLA_HV))
    return o.reshape(B, S, GLA_DV).astype(h.dtype) @ w_out


def conv_ffn(h, w_up, conv_w, conv_b, w_down):
    a, u = jnp.split(h @ w_up, 2, axis=-1)
    a = lax.conv_general_dilated(a, conv_w[:, None, :], window_strides=(1,),
                                 padding=[(CONV_WIDTH - 1, 0)],
                                 dimension_numbers=('NWC', 'WIO', 'NWC'),
                                 feature_group_count=FFN_DIM) + conv_b
    return (jax.nn.silu(a) * u) @ w_down


def setup_inputs(seed: int = 0) -> dict:
    key = jax.random.key(seed)
    ks = jax.random.split(key, 20)
    D, F = D_MODEL, FFN_DIM

    def nrm(k, shape, scale):
        return jax.random.normal(k, shape, jnp.float32) * scale

    return {
        "x": nrm(ks[0], (BATCH, SEQ, D), 1.0),
        "c": nrm(ks[1], (BATCH, D), 1.0),
        "w_ada": nrm(ks[2], (DEPTH, D, 6 * D), 0.5 * D ** -0.5),
        "b_ada": nrm(ks[3], (DEPTH, 6 * D), 0.01),
        "norm_mix": 1.0 + nrm(ks[4], (DEPTH, D), 0.01),
        "norm_ffn": 1.0 + nrm(ks[5], (DEPTH, D), 0.01),
        "attn_w_in": nrm(ks[6], (N_ATTN_LAYERS, D, DSW_IN_DIM), D ** -0.5),
        "attn_w_out": nrm(ks[7], (N_ATTN_LAYERS, D, D), D ** -0.5),
        "gla_w_in": nrm(ks[8], (N_GLA_LAYERS, D, GLA_IN_DIM), D ** -0.5),
        "gla_w_gate": nrm(ks[9], (N_GLA_LAYERS, GLA_GATE_RANK, GLA_DK), GLA_GATE_RANK ** -0.5),
        "gla_b_gate": nrm(ks[10], (N_GLA_LAYERS, GLA_DK), 0.01),
        "gla_norm": 1.0 + nrm(ks[11], (N_GLA_LAYERS, GLA_HV), 0.01),
        "gla_w_out": nrm(ks[12], (N_GLA_LAYERS, GLA_DV, D), GLA_DV ** -0.5),
        "ffn_w_up": nrm(ks[13], (DEPTH, D, 2 * F), D ** -0.5),
        "ffn_conv_w": nrm(ks[14], (DEPTH, CONV_WIDTH, F), CONV_WIDTH ** -0.5),
        "ffn_conv_b": nrm(ks[15], (DEPTH, F), 0.01),
        "ffn_w_down": nrm(ks[16], (DEPTH, F, D), F ** -0.5),
        "norm_final": 1.0 + nrm(ks[17], (D,), 0.01),
    }


def reference(x, c, w_ada, b_ada, norm_mix, norm_ffn, attn_w_in, attn_w_out,
              gla_w_in, gla_w_gate, gla_b_gate, gla_norm, gla_w_out,
              ffn_w_up, ffn_conv_w, ffn_conv_b, ffn_w_down, norm_final):
    cond = jax.nn.silu(c)
    for i in range(DEPTH):
        mod = cond @ w_ada[i] + b_ada[i]
        sh1, sc1, g1, sh2, sc2, g2 = jnp.split(mod, 6, axis=-1)
        h = modulate(x, norm_mix[i], sh1, sc1)
        j = i // N_MIXERS
        if i % N_MIXERS == 0:
            y = dilated_mixture_attention(h, attn_w_in[j], attn_w_out[j])
        else:
            y = gated_linear_attention(h, gla_w_in[j], gla_w_gate[j], gla_b_gate[j],
                                       gla_norm[j], gla_w_out[j])
        x = x + g1[:, None, :] * y
        h = modulate(x, norm_ffn[i], sh2, sc2)
        x = x + g2[:, None, :] * conv_ffn(h, ffn_w_up[i], ffn_conv_w[i], ffn_conv_b[i], ffn_w_down[i])
    return rms_norm(x, norm_final)
```

```python
import functools

import jax
import jax.numpy as jnp
from jax import lax
from jax.experimental import pallas as pl
from jax.experimental.pallas import tpu as pltpu

F32 = jnp.float32
BF16 = jnp.bfloat16

D_MODEL = 1024
EPS = 1e-6

DSW_PATTERNS = ((128, 1), (512, 4), (2048, 16))
DSW_GROUPS = len(DSW_PATTERNS)
DSW_HEADS = 16
DSW_HEAD_DIM = 64
DSW_BLOCK = 128
HEAD_PAIRS = DSW_HEADS // 2
LANES = 128

GLA_HEADS = 4
GLA_DK = 512
GLA_DV = 1024
GLA_HK = 128
GLA_HV = 256
GLA_RANK = 16
GLA_TAU = 16.0
GLA_CHUNK = 64
GLA_TILE = 256

FFN_DIM = 2816
FFN_CHUNK = 256
CONV_HALO = 8

VMEM_LIMIT_BYTES = 56 * 1024 * 1024


def _params(*semantics):
    return pltpu.CompilerParams(dimension_semantics=semantics,
                                vmem_limit_bytes=VMEM_LIMIT_BYTES)


def _rms(x):
    return x * lax.rsqrt(jnp.mean(x * x, axis=-1, keepdims=True) + EPS)


def _modulate(x, gain, shift, scale):
    return (_rms(x) * gain) * (1.0 + scale) + shift


def _silu(x):
    return x * jax.nn.sigmoid(x)


def _dot(a, b):
    return jnp.dot(a, b, preferred_element_type=F32)


def _dot_nt(a, b):
    return lax.dot_general(a, b, (((1,), (1,)), ((), ())), preferred_element_type=F32)


def _dot_tn(a, b):
    return lax.dot_general(a, b, (((0,), (0,)), ((), ())), preferred_element_type=F32)


def _ada_kernel(c_ref, w_ref, b_ref, o_ref):
    cond = _silu(c_ref[...])
    o_ref[...] = _dot(cond, w_ref[...]) + b_ref[...]


def _ada(c, w_ada, b_ada, tn=1536):
    depth, d, n = w_ada.shape
    batch = c.shape[0]
    return pl.pallas_call(
        _ada_kernel,
        grid=(depth, n // tn),
        in_specs=[
            pl.BlockSpec((batch, d), lambda l, j: (0, 0)),
            pl.BlockSpec((None, d, tn), lambda l, j: (l, 0, j)),
            pl.BlockSpec((None, 1, tn), lambda l, j: (l, 0, j)),
        ],
        out_specs=pl.BlockSpec((None, batch, tn), lambda l, j: (l, 0, j)),
        out_shape=jax.ShapeDtypeStruct((depth, batch, n), F32),
        compiler_params=_params("parallel", "parallel"),
        name="ada_mod",
    )(c, w_ada, b_ada.reshape(depth, 1, n))


def _inproj_kernel(x_ref, mod_ref, gain_ref, w_ref, o_ref, h_ref):
    @pl.when(pl.program_id(2) == 0)
    def _():
        h = _modulate(x_ref[...], gain_ref[...], mod_ref[0:1, :], mod_ref[1:2, :])
        h_ref[...] = h.astype(BF16)

    o_ref[...] = _dot(h_ref[...], w_ref[...]).astype(o_ref.dtype)


def _inproj(x, mod, gain, w, tm=1024, tn=1024):
    batch, seq, d = x.shape
    n = w.shape[1]
    return pl.pallas_call(
        _inproj_kernel,
        grid=(batch, seq // tm, n // tn),
        in_specs=[
            pl.BlockSpec((None, tm, d), lambda b, i, j: (b, i, 0)),
            pl.BlockSpec((None, 6, d), lambda b, i, j: (b, 0, 0)),
            pl.BlockSpec((1, d), lambda b, i, j: (0, 0)),
            pl.BlockSpec((d, tn), lambda b, i, j: (0, j)),
        ],
        out_specs=pl.BlockSpec((None, tm, tn), lambda b, i, j: (b, i, j)),
        out_shape=jax.ShapeDtypeStruct((batch, seq, n), BF16),
        scratch_shapes=[pltpu.VMEM((tm, d), BF16)],
        compiler_params=_params("parallel", "parallel", "arbitrary"),
        name="attn_inproj",
    )(x, mod, gain, w)


def _attn_kernel(*refs, first, last):
    if first:
        q_ref, kp_ref, kc_ref, vp_ref, vc_ref = refs[:5]
        outs = refs[5:]
    else:
        q_ref, kp_ref, kc_ref, vp_ref, vc_ref, op_ref, lp_ref = refs[:7]
        outs = refs[7:]
    o_ref = outs[0]
    lse_ref = None if last else outs[1]

    blk = DSW_BLOCK
    n = pl.program_id(2)
    row = lax.broadcasted_iota(jnp.int32, (blk, 2 * blk), 0)
    col = lax.broadcasted_iota(jnp.int32, (blk, 2 * blk), 1)
    lo = jnp.where(n == 0, blk, 0)
    valid = (col >= jnp.maximum(row, lo)) & (col <= row + blk)
    bias = jnp.where(valid, 0.0, -jnp.inf).astype(F32)

    lane = lax.broadcasted_iota(jnp.int32, (blk, LANES), 1)
    left = lane < DSW_HEAD_DIM
    scale = DSW_HEAD_DIM ** -0.5

    lse_tile = jnp.zeros((blk, LANES), F32)
    if not first:
        lse_prev = lp_ref[...]

    for j in range(HEAD_PAIRS):
        sl = slice(j * LANES, (j + 1) * LANES)
        q2 = q_ref[:, sl]
        k2 = jnp.concatenate([kp_ref[:, sl], kc_ref[:, sl]], axis=0)
        v2 = jnp.concatenate([vp_ref[:, sl], vc_ref[:, sl]], axis=0)
        pv, fac_prev = [], []
        for side in range(2):
            h = 2 * j + side
            keep = left if side == 0 else jnp.logical_not(left)
            qm = jnp.where(keep, q2, jnp.zeros_like(q2)) * scale
            s = _dot_nt(qm, k2) + bias
            m = jnp.max(s, axis=-1, keepdims=True)
            p = jnp.exp(s - m)
            l = jnp.sum(p, axis=-1, keepdims=True)
            acc = _dot(p.astype(BF16), v2)
            lse = m + jnp.log(l)
            if first:
                pv.append(acc / l)
                new = lse
            else:
                old = lse_prev[:, h:h + 1]
                mx = jnp.maximum(old, lse)
                new = mx + jnp.log(jnp.exp(old - mx) + jnp.exp(lse - mx))
                pv.append(acc * jnp.exp(m - new))
                fac_prev.append(jnp.exp(old - new))
            if not last:
                lse_tile = jnp.where(lane == h, new, lse_tile)
        o2 = jnp.where(left, pv[0], pv[1])
        if not first:
            o2 = o2 + op_ref[:, sl] * jnp.where(left, fac_prev[0], fac_prev[1])
        o_ref[:, sl] = o2.astype(o_ref.dtype)

    if not last:
        lse_ref[...] = lse_tile


def _attn_group(qkv, g, dilation, o_prev, lse_prev, last):
    batch, seq, n_in = qkv.shape
    d = D_MODEL
    blk = DSW_BLOCK
    rows = seq // dilation
    nb = rows // blk
    first = o_prev is None
    cols = n_in // d
    view = qkv.reshape(batch, rows, dilation * n_in)

    def spec(t, prev):
        if prev:
            return pl.BlockSpec((None, blk, d),
                                lambda b, r, n: (b, jnp.maximum(n - 1, 0), r * cols + t * DSW_GROUPS + g))
        return pl.BlockSpec((None, blk, d), lambda b, r, n: (b, n, r * cols + t * DSW_GROUPS + g))

    o_spec = pl.BlockSpec((None, blk, d), lambda b, r, n: (b, n, r))
    l_spec = pl.BlockSpec((None, blk, LANES), lambda b, r, n: (b, n, r))
    in_specs = [spec(0, False), spec(1, True), spec(1, False), spec(2, True), spec(2, False)]
    args = [view] * 5
    if not first:
        in_specs += [o_spec, l_spec]
        args += [o_prev.reshape(batch, rows, dilation * d),
                 lse_prev.reshape(batch, rows, dilation * LANES)]
    o_shape = jax.ShapeDtypeStruct((batch, rows, dilation * d), BF16 if last else F32)
    l_shape = jax.ShapeDtypeStruct((batch, rows, dilation * LANES), F32)
    res = pl.pallas_call(
        functools.partial(_attn_kernel, first=first, last=last),
        grid=(batch, dilation, nb),
        in_specs=in_specs,
        out_specs=o_spec if last else [o_spec, l_spec],
        out_shape=o_shape if last else [o_shape, l_shape],
        compiler_params=_params("parallel", "parallel", "arbitrary"),
        name=f"attn_group{g}",
    )(*args)
    if last:
        return res.reshape(batch, seq, d), None
    return res[0].reshape(batch, seq, d), res[1].reshape(batch, seq, LANES)


def _outproj_kernel(o_ref, w_ref, x_ref, mod_ref, out_ref):
    y = _dot(o_ref[...], w_ref[...])
    out_ref[...] = x_ref[...] + mod_ref[2:3, :] * y


def _outproj(o, w, x, mod, tm=1024):
    batch, seq, d = x.shape
    k = o.shape[-1]
    return pl.pallas_call(
        _outproj_kernel,
        grid=(batch, seq // tm),
        in_specs=[
            pl.BlockSpec((None, tm, k), lambda b, i: (b, i, 0)),
            pl.BlockSpec((k, d), lambda b, i: (0, 0)),
            pl.BlockSpec((None, tm, d), lambda b, i: (b, i, 0)),
            pl.BlockSpec((None, 6, d), lambda b, i: (b, 0, 0)),
        ],
        out_specs=pl.BlockSpec((None, tm, d), lambda b, i: (b, i, 0)),
        out_shape=jax.ShapeDtypeStruct((batch, seq, d), F32),
        compiler_params=_params("parallel", "parallel"),
        name="outproj",
    )(o, w, x, mod)


def _ffn_kernel(*refs, final):
    if final:
        x_ref, mod_ref, gain_ref, wup_ref, cw_ref, cb_ref, wdn_ref, fg_ref, out_ref, carry_ref = refs
    else:
        x_ref, mod_ref, gain_ref, wup_ref, cw_ref, cb_ref, wdn_ref, out_ref, carry_ref = refs
    tm = x_ref.shape[0]
    tf = FFN_CHUNK

    @pl.when(pl.program_id(1) == 0)
    def _():
        carry_ref[...] = jnp.zeros_like(carry_ref)

    x = x_ref[...]
    h = _modulate(x, gain_ref[...], mod_ref[3:4, :], mod_ref[4:5, :]).astype(BF16)
    row = lax.broadcasted_iota(jnp.int32, (tm, tf), 0)
    acc = jnp.zeros((tm, D_MODEL), F32)
    for c in range(FFN_DIM // tf):
        cs = slice(c * tf, (c + 1) * tf)
        us = slice(FFN_DIM + c * tf, FFN_DIM + (c + 1) * tf)
        a = _dot(h, wup_ref[:, cs])
        u = _dot(h, wup_ref[:, us])
        prev = carry_ref[:, cs]
        a1 = jnp.where(row == 0, prev[CONV_HALO - 1:CONV_HALO, :], pltpu.roll(a, 1, axis=0))
        a2 = jnp.where(row == 0, prev[CONV_HALO - 2:CONV_HALO - 1, :],
                       jnp.where(row == 1, prev[CONV_HALO - 1:CONV_HALO, :],
                                 pltpu.roll(a, 2, axis=0)))
        carry_ref[:, cs] = a[tm - CONV_HALO:tm, :]
        ac = cw_ref[2:3, cs] * a + cw_ref[1:2, cs] * a1 + cw_ref[0:1, cs] * a2 + cb_ref[:, cs]
        act = (_silu(ac) * u).astype(BF16)
        acc = acc + _dot(act, wdn_ref[cs, :])
    y = x + mod_ref[5:6, :] * acc
    if final:
        y = _rms(y) * fg_ref[...]
    out_ref[...] = y


def _ffn(x, mod, gain, w_up, conv_w, conv_b, w_down, final_gain=None, tm=512):
    batch, seq, d = x.shape
    f = FFN_DIM
    final = final_gain is not None
    const2 = lambda b, i: (0, 0)
    resident = dict(pipeline_mode=pl.Buffered(1))
    in_specs = [
        pl.BlockSpec((None, tm, d), lambda b, i: (b, i, 0)),
        pl.BlockSpec((None, 6, d), lambda b, i: (b, 0, 0)),
        pl.BlockSpec((1, d), const2),
        pl.BlockSpec((d, 2 * f), const2, **resident),
        pl.BlockSpec((3, f), const2),
        pl.BlockSpec((1, f), const2),
        pl.BlockSpec((f, d), const2, **resident),
    ]
    args = [x, mod, gain, w_up, conv_w, conv_b, w_down]
    if final:
        in_specs.append(pl.BlockSpec((1, d), const2))
        args.append(final_gain)
    return pl.pallas_call(
        functools.partial(_ffn_kernel, final=final),
        grid=(batch, seq // tm),
        in_specs=in_specs,
        out_specs=pl.BlockSpec((None, tm, d), lambda b, i: (b, i, 0)),
        out_shape=jax.ShapeDtypeStruct((batch, seq, d), F32),
        scratch_shapes=[pltpu.VMEM((CONV_HALO, f), F32)],
        compiler_params=_params("arbitrary", "arbitrary"),
        name="conv_ffn",
    )(*args)


def _gla_inproj_kernel(x_ref, mod_ref, gain_ref, w_ref, wl_ref, wg_ref, bg_ref,
                       q_ref, k_ref, v_ref, r_ref, la_ref):
    h = _modulate(x_ref[...], gain_ref[...], mod_ref[0:1, :], mod_ref[1:2, :]).astype(BF16)
    q_ref[...] = _dot(h, w_ref[:, 0:GLA_DK])
    k_ref[...] = _dot(h, w_ref[:, GLA_DK:2 * GLA_DK])
    v_ref[...] = _dot(h, w_ref[:, 2 * GLA_DK:2 * GLA_DK + GLA_DV]).astype(BF16)
    r_ref[...] = _dot(h, w_ref[:, 2 * GLA_DK + GLA_DV:])
    g_low = _dot(h, wl_ref[...])
    z = _dot(g_low.astype(BF16), wg_ref[...]) + bg_ref[...]
    log_sig = jnp.minimum(z, 0.0) - jnp.log(1.0 + jnp.exp(-jnp.abs(z)))
    la_ref[...] = log_sig / GLA_TAU


def _gla_inproj(x, mod, gain, w_main, w_low, w_gate, b_gate, tm=512):
    batch, seq, d = x.shape
    const2 = lambda b, i: (0, 0)
    row_spec = lambda n: pl.BlockSpec((None, tm, n), lambda b, i: (b, i, 0))
    shape = lambda n, dt: jax.ShapeDtypeStruct((batch, seq, n), dt)
    return pl.pallas_call(
        _gla_inproj_kernel,
        grid=(batch, seq // tm),
        in_specs=[
            row_spec(d),
            pl.BlockSpec((None, 6, d), lambda b, i: (b, 0, 0)),
            pl.BlockSpec((1, d), const2),
            pl.BlockSpec(w_main.shape, const2),
            pl.BlockSpec(w_low.shape, const2),
            pl.BlockSpec(w_gate.shape, const2),
            pl.BlockSpec((1, GLA_DK), const2),
        ],
        out_specs=[row_spec(GLA_DK), row_spec(GLA_DK), row_spec(GLA_DV), row_spec(GLA_DV),
                   row_spec(GLA_DK)],
        out_shape=[shape(GLA_DK, F32), shape(GLA_DK, F32), shape(GLA_DV, BF16),
                   shape(GLA_DV, F32), shape(GLA_DK, F32)],
        compiler_params=_params("parallel", "parallel"),
        name="gla_inproj",
    )(x, mod, gain, w_main, w_low, w_gate, b_gate)


def _gla_kernel(q_ref, k_ref, v_ref, r_ref, la_ref, gain_ref, tri_ref, o_ref, st_ref):
    @pl.when(pl.program_id(1) == 0)
    def _():
        st_ref[...] = jnp.zeros_like(st_ref)

    ck = GLA_CHUNK
    la = la_ref[...]
    la_hi = la.astype(BF16)
    la_lo = (la - la_hi.astype(F32)).astype(BF16)
    tri = tri_ref[...]
    b = _dot(tri, la_hi) + _dot(tri, la_lo)
    ri = lax.broadcasted_iota(jnp.int32, (ck, ck), 0)
    ci = lax.broadcasted_iota(jnp.int32, (ck, ck), 1)
    causal = ri >= ci
    gain = gain_ref[...]
    scale = GLA_HK ** -0.5

    for c in range(GLA_TILE // ck):
        rs = slice(c * ck, (c + 1) * ck)
        bc = b[rs, :]
        bl = bc[ck - 1:ck, :]
        kc = k_ref[rs, :]
        qd = (q_ref[rs, :] * scale * jnp.exp(bc)).astype(BF16)
        kd = (kc * jnp.exp(-bc)).astype(BF16)
        ke = (kc * jnp.exp(bl - bc)).astype(BF16)
        dec = jnp.exp(bl)
        for hd in range(GLA_HEADS):
            ks = slice(hd * GLA_HK, (hd + 1) * GLA_HK)
            vs = slice(hd * GLA_HV, (hd + 1) * GLA_HV)
            vc = v_ref[rs, vs]
            att = jnp.where(causal, _dot_nt(qd[:, ks], kd[:, ks]), 0.0)
            st = st_ref[hd]
            o = _dot(att.astype(BF16), vc) + _dot_nt(qd[:, ks], st.astype(BF16))
            st_ref[hd] = st * dec[:, ks] + _dot_tn(vc, ke[:, ks])
            rr = r_ref[rs, vs]
            o_ref[rs, vs] = ((_rms(o) * gain) * _silu(rr)).astype(o_ref.dtype)


def _gla(q, k, v, r, log_a, gain):
    batch, seq, _ = q.shape
    t = GLA_TILE
    idx = jnp.arange(t)
    tri = ((idx[:, None] >= idx[None, :]) &
           (idx[:, None] // GLA_CHUNK == idx[None, :] // GLA_CHUNK)).astype(BF16)
    row_spec = lambda n: pl.BlockSpec((None, t, n), lambda b, i: (b, i, 0))
    const2 = lambda b, i: (0, 0)
    return pl.pallas_call(
        _gla_kernel,
        grid=(batch, seq // t),
        in_specs=[row_spec(GLA_DK), row_spec(GLA_DK), row_spec(GLA_DV), row_spec(GLA_DV),
                  row_spec(GLA_DK), pl.BlockSpec((1, GLA_HV), const2), pl.BlockSpec((t, t), const2)],
        out_specs=row_spec(GLA_DV),
        out_shape=jax.ShapeDtypeStruct((batch, seq, GLA_DV), BF16),
        scratch_shapes=[pltpu.VMEM((GLA_HEADS, GLA_HV, GLA_HK), F32)],
        compiler_params=_params("arbitrary", "arbitrary"),
        name="gla_recurrence",
    )(q, k, v, r, log_a, gain, tri)


def kernel(x, c, w_ada, b_ada, norm_mix, norm_ffn, attn_w_in, attn_w_out,
           gla_w_in, gla_w_gate, gla_b_gate, gla_norm, gla_w_out,
           ffn_w_up, ffn_conv_w, ffn_conv_b, ffn_w_down, norm_final):
    batch, seq, d = x.shape
    depth = w_ada.shape[0]
    mods = _ada(c, w_ada, b_ada).reshape(depth, batch, 6, d)

    mod = mods[0]
    qkv = _inproj(x, mod, norm_mix[0].reshape(1, d), attn_w_in[0].astype(BF16))
    o, lse = None, None
    for g, (_, dilation) in enumerate(DSW_PATTERNS):
        o, lse = _attn_group(qkv, g, dilation, o, lse, last=(g == DSW_GROUPS - 1))
    x = _outproj(o, attn_w_out[0].astype(BF16), x, mod)
    x = _ffn(x, mod, norm_ffn[0].reshape(1, d), ffn_w_up[0].astype(BF16), ffn_conv_w[0],
             ffn_conv_b[0].reshape(1, FFN_DIM), ffn_w_down[0].astype(BF16))

    mod = mods[1]
    w_in = gla_w_in[0]
    n_main = 2 * GLA_DK + GLA_DV
    w_main = jnp.concatenate([w_in[:, :n_main], w_in[:, n_main + GLA_RANK:]], axis=1).astype(BF16)
    w_low = w_in[:, n_main:n_main + GLA_RANK].astype(BF16)
    q, k, v, r, log_a = _gla_inproj(x, mod, norm_mix[1].reshape(1, d), w_main, w_low,
                                    gla_w_gate[0].astype(BF16), gla_b_gate[0].reshape(1, GLA_DK))
    o = _gla(q, k, v, r, log_a, gla_norm[0].reshape(1, GLA_HV))
    x = _outproj(o, gla_w_out[0].astype(BF16), x, mod)
    x = _ffn(x, mod, norm_ffn[1].reshape(1, d), ffn_w_up[1].astype(BF16), ffn_conv_w[1],
             ffn_conv_b[1].reshape(1, FFN_DIM), ffn_w_down[1].astype(BF16),
             final_gain=norm_final.reshape(1, d))
    return x
```

```python
import functools
import math

import jax
import jax.numpy as jnp
from jax import lax
from jax.experimental import pallas as pl
from jax.experimental.pallas import tpu as pltpu

F32 = jnp.float32
BF16 = jnp.bfloat16

D_MODEL = 1024
EPS = 1e-6
LANES = 128

DSW_DILATIONS = (1, 4, 16)
DSW_GROUPS = len(DSW_DILATIONS)
DSW_HEADS = 16
DSW_HEAD_DIM = 64
DSW_BLOCK = 128
HEAD_PAIRS = DSW_HEADS // 2
DSW_UNIT = 1024
DSW_TILE = 2048

GLA_HEADS = 4
GLA_DK = 512
GLA_DV = 1024
GLA_HK = 128
GLA_HV = 256
GLA_RANK = 16
GLA_TAU = 16.0
GLA_CHUNK = 64
GLA_TILE = 256

FFN_DIM = 2816
FFN_CHUNK = 256
CONV_HALO = 8

VMEM_LIMIT_BYTES = 56 * 1024 * 1024


def _params(*semantics):
    return pltpu.CompilerParams(dimension_semantics=semantics,
                                vmem_limit_bytes=VMEM_LIMIT_BYTES)


def _rms(x):
    return x * lax.rsqrt(jnp.mean(x * x, axis=-1, keepdims=True) + EPS)


def _modulate(x, gain, shift, scale):
    return (_rms(x) * gain) * (1.0 + scale) + shift


def _silu(x):
    return x * jax.nn.sigmoid(x)


def _dot(a, b):
    return jnp.dot(a, b, preferred_element_type=F32)


def _dot_nt(a, b):
    return lax.dot_general(a, b, (((1,), (1,)), ((), ())), preferred_element_type=F32)


def _dot_tn(a, b):
    return lax.dot_general(a, b, (((0,), (0,)), ((), ())), preferred_element_type=F32)


def _ada_kernel(c_ref, w_ref, b_ref, o_ref):
    cond = _silu(c_ref[...])
    o_ref[...] = _dot(cond, w_ref[...]) + b_ref[...]


def _ada(c, w_ada, b_ada, tn=1536):
    depth, d, n = w_ada.shape
    batch = c.shape[0]
    return pl.pallas_call(
        _ada_kernel,
        grid=(depth, n // tn),
        in_specs=[
            pl.BlockSpec((batch, d), lambda l, j: (0, 0)),
            pl.BlockSpec((None, d, tn), lambda l, j: (l, 0, j)),
            pl.BlockSpec((None, 1, tn), lambda l, j: (l, 0, j)),
        ],
        out_specs=pl.BlockSpec((None, batch, tn), lambda l, j: (l, 0, j)),
        out_shape=jax.ShapeDtypeStruct((depth, batch, n), F32),
        compiler_params=_params("parallel", "parallel"),
        name="ada_mod",
    )(c, w_ada, b_ada.reshape(depth, 1, n))


def _attn_inproj_kernel(x_ref, mod_ref, gain_ref, w_ref, o_ref, h_ref, lane_ref):
    j = pl.program_id(2)
    tm = x_ref.shape[0]

    @pl.when(j == 0)
    def _():
        h = _modulate(x_ref[...], gain_ref[...], mod_ref[0:1, :], mod_ref[1:2, :])
        h_ref[0] = h.astype(BF16)
        for c in range(D_MODEL // LANES):
            lane_ref[c] = h[:, c * LANES:(c + 1) * LANES]
        for g in range(1, DSW_GROUPS):
            d = DSW_DILATIONS[g]
            n = tm // d
            for r in range(d):
                for c in range(D_MODEL // LANES):
                    h_ref[g, r * n:(r + 1) * n, c * LANES:(c + 1) * LANES] = (
                        lane_ref[c, pl.ds(r, n, stride=d), :].astype(BF16))

    res = _dot(h_ref[j // 3], w_ref[...])
    q_scale = DSW_HEAD_DIM ** -0.5 * math.log2(math.e)
    res = res * jnp.where(j % 3 == 0, q_scale, 1.0)
    for pp in range(HEAD_PAIRS):
        o_ref[pp] = res[:, pp * LANES:(pp + 1) * LANES].astype(BF16)


def _attn_inproj(x, mod, gain, w):
    batch, seq, d = x.shape
    tm = DSW_UNIT
    n_slots = 3 * DSW_GROUPS
    return pl.pallas_call(
        _attn_inproj_kernel,
        grid=(batch, seq // tm, n_slots),
        in_specs=[
            pl.BlockSpec((None, tm, d), lambda b, i, j: (b, i, 0)),
            pl.BlockSpec((None, 6, d), lambda b, i, j: (b, 0, 0)),
            pl.BlockSpec((1, d), lambda b, i, j: (0, 0)),
            pl.BlockSpec((d, d), lambda b, i, j: (0, j)),
        ],
        out_specs=pl.BlockSpec((None, HEAD_PAIRS, tm, LANES), lambda b, i, j: (b, j, i, 0)),
        out_shape=jax.ShapeDtypeStruct((batch, n_slots * HEAD_PAIRS, seq, LANES), BF16),
        scratch_shapes=[pltpu.VMEM((DSW_GROUPS, tm, d), BF16),
                        pltpu.VMEM((d // LANES, tm, LANES), F32)],
        compiler_params=_params("parallel", "parallel", "arbitrary"),
        name="attn_inproj",
    )(x, mod, gain, w)


def _attn_kernel(q0, k0p, k0, v0p, v0, q1, k1p, k1, v1p, v1, q2, k2p, k2, v2p, v2,
                 o_ref, pv0, m0, l0, pv1, m1, l1, pv2, m2, l2):
    blk = DSW_BLOCK
    tile = pl.program_id(1)
    row = lax.broadcasted_iota(jnp.int32, (blk, 2 * blk), 0)
    col = lax.broadcasted_iota(jnp.int32, (blk, 2 * blk), 1)
    band = (col >= row) & (col <= row + blk)
    bias_mid = jnp.where(band, 0.0, -jnp.inf).astype(F32)
    lo = jnp.where(tile == 0, blk, 0)
    bias_head = jnp.where(band & (col >= lo), 0.0, -jnp.inf).astype(F32)
    lane = lax.broadcasted_iota(jnp.int32, (blk, LANES), 1)
    left = lane < DSW_HEAD_DIM

    def band_block(q, kk, vv, bias):
        res = []
        for side in range(2):
            keep = left if side == 0 else jnp.logical_not(left)
            qm = jnp.where(keep, q, jnp.zeros_like(q))
            s = _dot_nt(qm, kk) + bias
            m = jnp.max(s, axis=-1, keepdims=True)
            p = jnp.exp2(s - m)
            l = jnp.sum(p, axis=-1, keepdims=True)
            res.append((_dot(p.astype(BF16), vv), m, l))
        return tuple(jnp.where(left, a, b) for a, b in zip(res[0], res[1]))

    def emit(refs, rows, q, kk, vv, bias):
        pv, m, l = band_block(q, kk, vv, bias)
        refs[0][rows, :] = pv
        refs[1][rows, :] = m
        refs[2][rows, :] = l

    cat = lambda parts: jnp.concatenate(parts, axis=0)

    g0 = (pv0, m0, l0)
    emit(g0, pl.ds(0, blk), q0[0:blk, :], cat([k0p[...], k0[0:blk, :]]),
         cat([v0p[...], v0[0:blk, :]]), bias_head)

    def body0(nb, carry):
        rows = pl.ds(pl.multiple_of(nb * blk, blk), blk)
        krows = pl.ds(pl.multiple_of((nb - 1) * blk, blk), 2 * blk)
        emit(g0, rows, q0[rows, :], k0[krows, :], v0[krows, :], bias_mid)
        return carry

    lax.fori_loop(1, DSW_TILE // blk, body0, 0)

    g1 = (pv1, m1, l1)
    d1 = DSW_DILATIONS[1]

    def body1(r, carry):
        for nb in range(4):
            u, hb = divmod(nb, 2)
            q = q1[u, r, hb * blk:(hb + 1) * blk, :]
            if nb == 0:
                kk = cat([k1p[r], k1[0, r, 0:blk, :]])
                vv = cat([v1p[r], v1[0, r, 0:blk, :]])
            elif nb == 2:
                kk = cat([k1[0, r, blk:2 * blk, :], k1[1, r, 0:blk, :]])
                vv = cat([v1[0, r, blk:2 * blk, :], v1[1, r, 0:blk, :]])
            else:
                kk = k1[u, r, :, :]
                vv = v1[u, r, :, :]
            rows = pl.ds(nb * blk * d1 + r, blk, stride=d1)
            emit(g1, rows, q, kk, vv, bias_head if nb == 0 else bias_mid)
        return carry

    lax.fori_loop(0, d1, body1, 0)

    g2 = (pv2, m2, l2)
    d2 = DSW_DILATIONS[2]

    def body2(r, carry):
        q = cat([q2[0, r], q2[1, r]])
        kk = cat([k2p[0, r], k2p[1, r], k2[0, r], k2[1, r]])
        vv = cat([v2p[0, r], v2p[1, r], v2[0, r], v2[1, r]])
        emit(g2, pl.ds(r, blk, stride=d2), q, kk, vv, bias_head)
        return carry

    lax.fori_loop(0, d2, body2, 0)

    def merge(i, carry):
        rows = pl.ds(pl.multiple_of(i * blk, blk), blk)
        ma, mb, mc = m0[rows, :], m1[rows, :], m2[rows, :]
        mx = jnp.maximum(jnp.maximum(ma, mb), mc)
        ea, eb, ec = jnp.exp2(ma - mx), jnp.exp2(mb - mx), jnp.exp2(mc - mx)
        num = pv0[rows, :] * ea + pv1[rows, :] * eb + pv2[rows, :] * ec
        den = l0[rows, :] * ea + l1[rows, :] * eb + l2[rows, :] * ec
        o_ref[rows, :] = (num / den).astype(o_ref.dtype)
        return carry

    lax.fori_loop(0, DSW_TILE // blk, merge, 0)


def _attention(qkv):
    batch, n_slots, seq, lanes = qkv.shape
    blk, unit, tile = DSW_BLOCK, DSW_UNIT, DSW_TILE
    n_tiles = seq // tile
    upt = tile // unit
    d1, d2 = DSW_DILATIONS[1], DSW_DILATIONS[2]

    def slot(g, t, j):
        return (g * 3 + t) * HEAD_PAIRS + j

    flat_blocks = qkv.reshape(batch, n_slots, seq // blk, blk, lanes)
    v1_cur = qkv.reshape(batch, n_slots, n_tiles, upt, d1, unit // d1, lanes)
    v1_prev = qkv.reshape(batch, n_slots, seq // unit, d1, unit // d1 // blk, blk, lanes)
    v2_cur = qkv.reshape(batch, n_slots, n_tiles, upt, d2, unit // d2, lanes)

    def specs(g, t):
        if g == 0:
            cur = pl.BlockSpec((None, None, tile, lanes), lambda b, i, j: (b, slot(0, t, j), i, 0))
            prev = pl.BlockSpec(
                (None, None, None, blk, lanes),
                lambda b, i, j: (b, slot(0, t, j), jnp.maximum(i * (tile // blk) - 1, 0), 0, 0))
            return (prev, flat_blocks), (cur, qkv)
        if g == 1:
            cur = pl.BlockSpec((None, None, None, upt, d1, unit // d1, lanes),
                               lambda b, i, j: (b, slot(1, t, j), i, 0, 0, 0, 0))
            prev = pl.BlockSpec(
                (None, None, None, d1, None, blk, lanes),
                lambda b, i, j: (b, slot(1, t, j), jnp.maximum(i * upt - 1, 0), 0,
                                 unit // d1 // blk - 1, 0, 0))
            return (prev, v1_prev), (cur, v1_cur)
        cur = pl.BlockSpec((None, None, None, upt, d2, unit // d2, lanes),
                           lambda b, i, j: (b, slot(2, t, j), i, 0, 0, 0, 0))
        prev = pl.BlockSpec((None, None, None, upt, d2, unit // d2, lanes),
                            lambda b, i, j: (b, slot(2, t, j), jnp.maximum(i - 1, 0), 0, 0, 0, 0))
        return (prev, v2_cur), (cur, v2_cur)

    in_specs, args = [], []
    for g in range(DSW_GROUPS):
        for t in range(3):
            prev, cur = specs(g, t)
            for spec, arr in ((prev, cur) if t else (cur,)):
                in_specs.append(spec)
                args.append(arr)
    return pl.pallas_call(
        _attn_kernel,
        grid=(batch, n_tiles, HEAD_PAIRS),
        in_specs=in_specs,
        out_specs=pl.BlockSpec((None, tile, lanes), lambda b, i, j: (b, i, j)),
        out_shape=jax.ShapeDtypeStruct((batch, seq, HEAD_PAIRS * lanes), BF16),
        scratch_shapes=[pltpu.VMEM((tile, lanes), F32)] * (3 * DSW_GROUPS),
        compiler_params=_params("parallel", "parallel", "parallel"),
        name="dilated_attention",
    )(*args)


def _outproj_kernel(o_ref, w_ref, x_ref, mod_ref, out_ref):
    y = _dot(o_ref[...], w_ref[...])
    out_ref[...] = x_ref[...] + mod_ref[2:3, :] * y


def _outproj(o, w, x, mod, tm=1024):
    batch, seq, d = x.shape
    k = o.shape[-1]
    return pl.pallas_call(
        _outproj_kernel,
        grid=(batch, seq // tm),
        in_specs=[
            pl.BlockSpec((None, tm, k), lambda b, i: (b, i, 0)),
            pl.BlockSpec((k, d), lambda b, i: (0, 0)),
            pl.BlockSpec((None, tm, d), lambda b, i: (b, i, 0)),
            pl.BlockSpec((None, 6, d), lambda b, i: (b, 0, 0)),
        ],
        out_specs=pl.BlockSpec((None, tm, d), lambda b, i: (b, i, 0)),
        out_shape=jax.ShapeDtypeStruct((batch, seq, d), F32),
        compiler_params=_params("parallel", "parallel"),
        name="outproj",
    )(o, w, x, mod)


def _ffn_kernel(*refs, final):
    if final:
        x_ref, mod_ref, gain_ref, wup_ref, cw_ref, cb_ref, wdn_ref, fg_ref, out_ref, carry_ref = refs
    else:
        x_ref, mod_ref, gain_ref, wup_ref, cw_ref, cb_ref, wdn_ref, out_ref, carry_ref = refs
    tm = x_ref.shape[0]
    tf = FFN_CHUNK

    @pl.when(pl.program_id(1) == 0)
    def _():
        carry_ref[...] = jnp.zeros_like(carry_ref)

    x = x_ref[...]
    h = _modulate(x, gain_ref[...], mod_ref[3:4, :], mod_ref[4:5, :]).astype(BF16)
    row = lax.broadcasted_iota(jnp.int32, (tm, tf), 0)
    acc = jnp.zeros((tm, D_MODEL), F32)
    for c in range(FFN_DIM // tf):
        cs = slice(c * tf, (c + 1) * tf)
        us = slice(FFN_DIM + c * tf, FFN_DIM + (c + 1) * tf)
        a = _dot(h, wup_ref[:, cs])
        u = _dot(h, wup_ref[:, us])
        prev = carry_ref[:, cs]
        a1 = jnp.where(row == 0, prev[CONV_HALO - 1:CONV_HALO, :], pltpu.roll(a, 1, axis=0))
        a2 = jnp.where(row == 0, prev[CONV_HALO - 2:CONV_HALO - 1, :],
                       jnp.where(row == 1, prev[CONV_HALO - 1:CONV_HALO, :],
                                 pltpu.roll(a, 2, axis=0)))
        carry_ref[:, cs] = a[tm - CONV_HALO:tm, :]
        ac = cw_ref[2:3, cs] * a + cw_ref[1:2, cs] * a1 + cw_ref[0:1, cs] * a2 + cb_ref[:, cs]
        act = (_silu(ac) * u).astype(BF16)
        acc = acc + _dot(act, wdn_ref[cs, :])
    y = x + mod_ref[5:6, :] * acc
    if final:
        y = _rms(y) * fg_ref[...]
    out_ref[...] = y


def _ffn(x, mod, gain, w_up, conv_w, conv_b, w_down, final_gain=None, tm=512):
    batch, seq, d = x.shape
    f = FFN_DIM
    final = final_gain is not None
    const2 = lambda b, i: (0, 0)
    resident = dict(pipeline_mode=pl.Buffered(1))
    in_specs = [
        pl.BlockSpec((None, tm, d), lambda b, i: (b, i, 0)),
        pl.BlockSpec((None, 6, d), lambda b, i: (b, 0, 0)),
        pl.BlockSpec((1, d), const2),
        pl.BlockSpec((d, 2 * f), const2, **resident),
        pl.BlockSpec((3, f), const2),
        pl.BlockSpec((1, f), const2),
        pl.BlockSpec((f, d), const2, **resident),
    ]
    args = [x, mod, gain, w_up, conv_w, conv_b, w_down]
    if final:
        in_specs.append(pl.BlockSpec((1, d), const2))
        args.append(final_gain)
    return pl.pallas_call(
        functools.partial(_ffn_kernel, final=final),
        grid=(batch, seq // tm),
        in_specs=in_specs,
        out_specs=pl.BlockSpec((None, tm, d), lambda b, i: (b, i, 0)),
        out_shape=jax.ShapeDtypeStruct((batch, seq, d), F32),
        scratch_shapes=[pltpu.VMEM((CONV_HALO, f), F32)],
        compiler_params=_params("arbitrary", "arbitrary"),
        name="conv_ffn",
    )(*args)


def _gla_inproj_kernel(x_ref, mod_ref, gain_ref, w_ref, wl_ref, wg_ref, bg_ref,
                       q_ref, k_ref, v_ref, r_ref, la_ref):
    h = _modulate(x_ref[...], gain_ref[...], mod_ref[0:1, :], mod_ref[1:2, :]).astype(BF16)
    q_ref[...] = _dot(h, w_ref[:, 0:GLA_DK])
    k_ref[...] = _dot(h, w_ref[:, GLA_DK:2 * GLA_DK])
    v_ref[...] = _dot(h, w_ref[:, 2 * GLA_DK:2 * GLA_DK + GLA_DV]).astype(BF16)
    r_ref[...] = _dot(h, w_ref[:, 2 * GLA_DK + GLA_DV:])
    g_low = _dot(h, wl_ref[...])
    z = _dot(g_low.astype(BF16), wg_ref[...]) + bg_ref[...]
    log_sig = jnp.minimum(z, 0.0) - jnp.log(1.0 + jnp.exp(-jnp.abs(z)))
    la_ref[...] = log_sig / GLA_TAU


def _gla_inproj(x, mod, gain, w_main, w_low, w_gate, b_gate, tm=512):
    batch, seq, d = x.shape
    const2 = lambda b, i: (0, 0)
    row_spec = lambda n: pl.BlockSpec((None, tm, n), lambda b, i: (b, i, 0))
    shape = lambda n, dt: jax.ShapeDtypeStruct((batch, seq, n), dt)
    return pl.pallas_call(
        _gla_inproj_kernel,
        grid=(batch, seq // tm),
        in_specs=[
            row_spec(d),
            pl.BlockSpec((None, 6, d), lambda b, i: (b, 0, 0)),
            pl.BlockSpec((1, d), const2),
            pl.BlockSpec(w_main.shape, const2),
            pl.BlockSpec(w_low.shape, const2),
            pl.BlockSpec(w_gate.shape, const2),
            pl.BlockSpec((1, GLA_DK), const2),
        ],
        out_specs=[row_spec(GLA_DK), row_spec(GLA_DK), row_spec(GLA_DV), row_spec(GLA_DV),
                   row_spec(GLA_DK)],
        out_shape=[shape(GLA_DK, F32), shape(GLA_DK, F32), shape(GLA_DV, BF16),
                   shape(GLA_DV, F32), shape(GLA_DK, F32)],
        compiler_params=_params("parallel", "parallel"),
        name="gla_inproj",
    )(x, mod, gain, w_main, w_low, w_gate, b_gate)


def _gla_kernel(q_ref, k_ref, v_ref, r_ref, la_ref, gain_ref, tri_ref, o_ref, st_ref):
    @pl.when(pl.program_id(1) == 0)
    def _():
        st_ref[...] = jnp.zeros_like(st_ref)

    ck = GLA_CHUNK
    la = la_ref[...]
    la_hi = la.astype(BF16)
    la_lo = (la - la_hi.astype(F32)).astype(BF16)
    tri = tri_ref[...]
    b = _dot(tri, la_hi) + _dot(tri, la_lo)
    ri = lax.broadcasted_iota(jnp.int32, (ck, ck), 0)
    ci = lax.broadcasted_iota(jnp.int32, (ck, ck), 1)
    causal = ri >= ci
    gain = gain_ref[...]
    scale = GLA_HK ** -0.5

    for c in range(GLA_TILE // ck):
        rs = slice(c * ck, (c + 1) * ck)
        bc = b[rs, :]
        bl = bc[ck - 1:ck, :]
        kc = k_ref[rs, :]
        qd = (q_ref[rs, :] * scale * jnp.exp(bc)).astype(BF16)
        kd = (kc * jnp.exp(-bc)).astype(BF16)
        ke = (kc * jnp.exp(bl - bc)).astype(BF16)
        dec = jnp.exp(bl)
        for hd in range(GLA_HEADS):
            ks = slice(hd * GLA_HK, (hd + 1) * GLA_HK)
            vs = slice(hd * GLA_HV, (hd + 1) * GLA_HV)
            vc = v_ref[rs, vs]
            att = jnp.where(causal, _dot_nt(qd[:, ks], kd[:, ks]), 0.0)
            st = st_ref[hd]
            o = _dot(att.astype(BF16), vc) + _dot_nt(qd[:, ks], st.astype(BF16))
            st_ref[hd] = st * dec[:, ks] + _dot_tn(vc, ke[:, ks])
            rr = r_ref[rs, vs]
            o_ref[rs, vs] = ((_rms(o) * gain) * _silu(rr)).astype(o_ref.dtype)


def _gla(q, k, v, r, log_a, gain):
    batch, seq, _ = q.shape
    t = GLA_TILE
    idx = jnp.arange(t)
    tri = ((idx[:, None] >= idx[None, :]) &
           (idx[:, None] // GLA_CHUNK == idx[None, :] // GLA_CHUNK)).astype(BF16)
    row_spec = lambda n: pl.BlockSpec((None, t, n), lambda b, i: (b, i, 0))
    const2 = lambda b, i: (0, 0)
    return pl.pallas_call(
        _gla_kernel,
        grid=(batch, seq // t),
        in_specs=[row_spec(GLA_DK), row_spec(GLA_DK), row_spec(GLA_DV), row_spec(GLA_DV),
                  row_spec(GLA_DK), pl.BlockSpec((1, GLA_HV), const2), pl.BlockSpec((t, t), const2)],
        out_specs=row_spec(GLA_DV),
        out_shape=jax.ShapeDtypeStruct((batch, seq, GLA_DV), BF16),
        scratch_shapes=[pltpu.VMEM((GLA_HEADS, GLA_HV, GLA_HK), F32)],
        compiler_params=_params("arbitrary", "arbitrary"),
        name="gla_recurrence",
    )(q, k, v, r, log_a, gain, tri)


def kernel(x, c, w_ada, b_ada, norm_mix, norm_ffn, attn_w_in, attn_w_out,
           gla_w_in, gla_w_gate, gla_b_gate, gla_norm, gla_w_out,
           ffn_w_up, ffn_conv_w, ffn_conv_b, ffn_w_down, norm_final):
    batch, seq, d = x.shape
    depth = w_ada.shape[0]
    mods = _ada(c, w_ada, b_ada).reshape(depth, batch, 6, d)

    mod = mods[0]
    w_qkv = attn_w_in[0].reshape(d, 3, DSW_GROUPS, d).transpose(0, 2, 1, 3)
    w_qkv = w_qkv.reshape(d, 3 * DSW_GROUPS * d).astype(BF16)
    qkv = _attn_inproj(x, mod, norm_mix[0].reshape(1, d), w_qkv)
    o = _attention(qkv)
    x = _outproj(o, attn_w_out[0].astype(BF16), x, mod)
    x = _ffn(x, mod, norm_ffn[0].reshape(1, d), ffn_w_up[0].astype(BF16), ffn_conv_w[0],
             ffn_conv_b[0].reshape(1, FFN_DIM), ffn_w_down[0].astype(BF16))

    mod = mods[1]
    w_in = gla_w_in[0]
    n_main = 2 * GLA_DK + GLA_DV
    w_main = jnp.concatenate([w_in[:, :n_main], w_in[:, n_main + GLA_RANK:]], axis=1).astype(BF16)
    w_low = w_in[:, n_main:n_main + GLA_RANK].astype(BF16)
    q, k, v, r, log_a = _gla_inproj(x, mod, norm_mix[1].reshape(1, d), w_main, w_low,
                                    gla_w_gate[0].astype(BF16), gla_b_gate[0].reshape(1, GLA_DK))
    o = _gla(q, k, v, r, log_a, gla_norm[0].reshape(1, GLA_HV))
    x = _outproj(o, gla_w_out[0].astype(BF16), x, mod)
    x = _ffn(x, mod, norm_ffn[1].reshape(1, d), ffn_w_up[1].astype(BF16), ffn_conv_w[1],
             ffn_conv_b[1].reshape(1, FFN_DIM), ffn_w_down[1].astype(BF16),
             final_gain=norm_final.reshape(1, d))
    return x
```

```python
import functools
import math

import jax
import jax.numpy as jnp
from jax import lax
from jax.experimental import pallas as pl
from jax.experimental.pallas import tpu as pltpu

F32 = jnp.float32
BF16 = jnp.bfloat16

D_MODEL = 1024
EPS = 1e-6
LANES = 128

DSW_DILATIONS = (1, 4, 16)
DSW_GROUPS = len(DSW_DILATIONS)
DSW_HEADS = 16
DSW_HEAD_DIM = 64
DSW_BLOCK = 128
HEAD_PAIRS = DSW_HEADS // 2
DSW_UNIT = 1024
DSW_TILE = 2048
G0_UNROLL = 5
G2_UNROLL = 4

GLA_HEADS = 4
GLA_DK = 512
GLA_DV = 1024
GLA_HK = 128
GLA_HV = 256
GLA_RANK = 16
GLA_TAU = 16.0
GLA_CHUNK = 64
GLA_TILE = 256

FFN_DIM = 2816
FFN_CHUNK = 256
CONV_HALO = 8

VMEM_LIMIT_BYTES = 56 * 1024 * 1024


def _params(*semantics):
    return pltpu.CompilerParams(dimension_semantics=semantics,
                                vmem_limit_bytes=VMEM_LIMIT_BYTES)


def _rms(x):
    return x * lax.rsqrt(jnp.mean(x * x, axis=-1, keepdims=True) + EPS)


def _modulate(x, gain, shift, scale):
    return (_rms(x) * gain) * (1.0 + scale) + shift


def _silu(x):
    return x * jax.nn.sigmoid(x)


def _dot(a, b):
    return jnp.dot(a, b, preferred_element_type=F32)


def _dot_nt(a, b):
    return lax.dot_general(a, b, (((1,), (1,)), ((), ())), preferred_element_type=F32)


def _dot_tn(a, b):
    return lax.dot_general(a, b, (((0,), (0,)), ((), ())), preferred_element_type=F32)


def _ada_kernel(c_ref, w_ref, b_ref, o_ref):
    cond = _silu(c_ref[...])
    o_ref[...] = _dot(cond, w_ref[...]) + b_ref[...]


def _ada(c, w_ada, b_ada, tn=1536):
    depth, d, n = w_ada.shape
    batch = c.shape[0]
    return pl.pallas_call(
        _ada_kernel,
        grid=(depth, n // tn),
        in_specs=[
            pl.BlockSpec((batch, d), lambda l, j: (0, 0)),
            pl.BlockSpec((None, d, tn), lambda l, j: (l, 0, j)),
            pl.BlockSpec((None, 1, tn), lambda l, j: (l, 0, j)),
        ],
        out_specs=pl.BlockSpec((None, batch, tn), lambda l, j: (l, 0, j)),
        out_shape=jax.ShapeDtypeStruct((depth, batch, n), F32),
        compiler_params=_params("parallel", "parallel"),
        name="ada_mod",
    )(c, w_ada, b_ada.reshape(depth, 1, n))


def _attn_inproj_kernel(x_ref, mod_ref, gain_ref, w_ref, o_ref, h_ref, lane_ref):
    j = pl.program_id(2)
    tm = x_ref.shape[0]

    @pl.when(j == 0)
    def _():
        h = _modulate(x_ref[...], gain_ref[...], mod_ref[0:1, :], mod_ref[1:2, :])
        h_ref[0] = h.astype(BF16)
        for c in range(D_MODEL // LANES):
            lane_ref[c] = h[:, c * LANES:(c + 1) * LANES]
        for g in range(1, DSW_GROUPS):
            d = DSW_DILATIONS[g]
            n = tm // d
            for r in range(d):
                for c in range(D_MODEL // LANES):
                    h_ref[g, r * n:(r + 1) * n, c * LANES:(c + 1) * LANES] = (
                        lane_ref[c, pl.ds(r, n, stride=d), :].astype(BF16))

    res = _dot(h_ref[j // 3], w_ref[...])
    q_scale = DSW_HEAD_DIM ** -0.5 * math.log2(math.e)
    res = res * jnp.where(j % 3 == 0, q_scale, 1.0)
    for pp in range(HEAD_PAIRS):
        o_ref[pp] = res[:, pp * LANES:(pp + 1) * LANES].astype(BF16)


def _attn_inproj(x, mod, gain, w):
    batch, seq, d = x.shape
    tm = DSW_UNIT
    n_slots = 3 * DSW_GROUPS
    return pl.pallas_call(
        _attn_inproj_kernel,
        grid=(batch, seq // tm, n_slots),
        in_specs=[
            pl.BlockSpec((None, tm, d), lambda b, i, j: (b, i, 0)),
            pl.BlockSpec((None, 6, d), lambda b, i, j: (b, 0, 0)),
            pl.BlockSpec((1, d), lambda b, i, j: (0, 0)),
            pl.BlockSpec((d, d), lambda b, i, j: (0, (j % 3) * DSW_GROUPS + j // 3)),
        ],
        out_specs=pl.BlockSpec((None, HEAD_PAIRS, tm, LANES), lambda b, i, j: (b, j, i, 0)),
        out_shape=jax.ShapeDtypeStruct((batch, n_slots * HEAD_PAIRS, seq, LANES), BF16),
        scratch_shapes=[pltpu.VMEM((DSW_GROUPS, tm, d), BF16),
                        pltpu.VMEM((d // LANES, tm, LANES), F32)],
        compiler_params=_params("parallel", "parallel", "arbitrary"),
        name="attn_inproj",
    )(x, mod, gain, w)


def _attn_kernel(q0, k0p, k0, v0p, v0, q1, k1p, k1, v1p, v1, q2, k2p, k2, v2p, v2,
                 o_ref, pv0, m0, l0, pv1, m1, l1, pv2, m2, l2):
    blk = DSW_BLOCK
    tile = pl.program_id(1)
    row = lax.broadcasted_iota(jnp.int32, (blk, 2 * blk), 0)
    col = lax.broadcasted_iota(jnp.int32, (blk, 2 * blk), 1)
    band = (col >= row) & (col <= row + blk)
    bias_mid = jnp.where(band, 0.0, -jnp.inf).astype(F32)
    lo = jnp.where(tile == 0, blk, 0)
    bias_head = jnp.where(band & (col >= lo), 0.0, -jnp.inf).astype(F32)
    lane = lax.broadcasted_iota(jnp.int32, (blk, LANES), 1)
    left = lane < DSW_HEAD_DIM

    def band_block(q, kk, vv, bias):
        res = []
        for side in range(2):
            keep = left if side == 0 else jnp.logical_not(left)
            qm = jnp.where(keep, q, jnp.zeros_like(q))
            s = _dot_nt(qm, kk) + bias
            m = jnp.max(s, axis=-1, keepdims=True)
            p = jnp.exp2(s - m)
            l = jnp.sum(p, axis=-1, keepdims=True)
            res.append((_dot(p.astype(BF16), vv), m, l))
        return tuple(jnp.where(left, a, b) for a, b in zip(res[0], res[1]))

    def emit(refs, rows, q, kk, vv, bias):
        pv, m, l = band_block(q, kk, vv, bias)
        refs[0][rows, :] = pv
        refs[1][rows, :] = m
        refs[2][rows, :] = l

    cat = lambda parts: jnp.concatenate(parts, axis=0)

    g0 = (pv0, m0, l0)
    emit(g0, pl.ds(0, blk), q0[0:blk, :], cat([k0p[...], k0[0:blk, :]]),
         cat([v0p[...], v0[0:blk, :]]), bias_head)

    def body0(i, carry):
        for t in range(G0_UNROLL):
            nb = 1 + i * G0_UNROLL + t
            rows = pl.ds(pl.multiple_of(nb * blk, blk), blk)
            krows = pl.ds(pl.multiple_of((nb - 1) * blk, blk), 2 * blk)
            emit(g0, rows, q0[rows, :], k0[krows, :], v0[krows, :], bias_mid)
        return carry

    lax.fori_loop(0, (DSW_TILE // blk - 1) // G0_UNROLL, body0, 0)

    g1 = (pv1, m1, l1)
    d1 = DSW_DILATIONS[1]

    def body1(r, carry):
        for nb in range(4):
            u, hb = divmod(nb, 2)
            q = q1[u, r, hb * blk:(hb + 1) * blk, :]
            if nb == 0:
                kk = cat([k1p[r], k1[0, r, 0:blk, :]])
                vv = cat([v1p[r], v1[0, r, 0:blk, :]])
            elif nb == 2:
                kk = cat([k1[0, r, blk:2 * blk, :], k1[1, r, 0:blk, :]])
                vv = cat([v1[0, r, blk:2 * blk, :], v1[1, r, 0:blk, :]])
            else:
                kk = k1[u, r, :, :]
                vv = v1[u, r, :, :]
            rows = pl.ds(nb * blk * d1 + r, blk, stride=d1)
            emit(g1, rows, q, kk, vv, bias_head if nb == 0 else bias_mid)
        return carry

    lax.fori_loop(0, d1, body1, 0)

    g2 = (pv2, m2, l2)
    d2 = DSW_DILATIONS[2]

    def body2(i, carry):
        for t in range(G2_UNROLL):
            r = i * G2_UNROLL + t
            q = cat([q2[0, r], q2[1, r]])
            kk = cat([k2p[0, r], k2p[1, r], k2[0, r], k2[1, r]])
            vv = cat([v2p[0, r], v2p[1, r], v2[0, r], v2[1, r]])
            emit(g2, pl.ds(r, blk, stride=d2), q, kk, vv, bias_head)
        return carry

    lax.fori_loop(0, d2 // G2_UNROLL, body2, 0)

    def merge(i, carry):
        rows = pl.ds(pl.multiple_of(i * blk, blk), blk)
        ma, mb, mc = m0[rows, :], m1[rows, :], m2[rows, :]
        mx = jnp.maximum(jnp.maximum(ma, mb), mc)
        ea, eb, ec = jnp.exp2(ma - mx), jnp.exp2(mb - mx), jnp.exp2(mc - mx)
        num = pv0[rows, :] * ea + pv1[rows, :] * eb + pv2[rows, :] * ec
        den = l0[rows, :] * ea + l1[rows, :] * eb + l2[rows, :] * ec
        o_ref[rows, :] = (num / den).astype(o_ref.dtype)
        return carry

    lax.fori_loop(0, DSW_TILE // blk, merge, 0)


def _attention(qkv):
    batch, n_slots, seq, lanes = qkv.shape
    blk, unit, tile = DSW_BLOCK, DSW_UNIT, DSW_TILE
    n_tiles = seq // tile
    upt = tile // unit
    d1, d2 = DSW_DILATIONS[1], DSW_DILATIONS[2]

    def slot(g, t, j):
        return (g * 3 + t) * HEAD_PAIRS + j

    flat_blocks = qkv.reshape(batch, n_slots, seq // blk, blk, lanes)
    v1_cur = qkv.reshape(batch, n_slots, n_tiles, upt, d1, unit // d1, lanes)
    v1_prev = qkv.reshape(batch, n_slots, seq // unit, d1, unit // d1 // blk, blk, lanes)
    v2_cur = qkv.reshape(batch, n_slots, n_tiles, upt, d2, unit // d2, lanes)

    def specs(g, t):
        if g == 0:
            cur = pl.BlockSpec((None, None, tile, lanes), lambda b, i, j: (b, slot(0, t, j), i, 0))
            prev = pl.BlockSpec(
                (None, None, None, blk, lanes),
                lambda b, i, j: (b, slot(0, t, j), jnp.maximum(i * (tile // blk) - 1, 0), 0, 0))
            return (prev, flat_blocks), (cur, qkv)
        if g == 1:
            cur = pl.BlockSpec((None, None, None, upt, d1, unit // d1, lanes),
                               lambda b, i, j: (b, slot(1, t, j), i, 0, 0, 0, 0))
            prev = pl.BlockSpec(
                (None, None, None, d1, None, blk, lanes),
                lambda b, i, j: (b, slot(1, t, j), jnp.maximum(i * upt - 1, 0), 0,
                                 unit // d1 // blk - 1, 0, 0))
            return (prev, v1_prev), (cur, v1_cur)
        cur = pl.BlockSpec((None, None, None, upt, d2, unit // d2, lanes),
                           lambda b, i, j: (b, slot(2, t, j), i, 0, 0, 0, 0))
        prev = pl.BlockSpec((None, None, None, upt, d2, unit // d2, lanes),
                            lambda b, i, j: (b, slot(2, t, j), jnp.maximum(i - 1, 0), 0, 0, 0, 0))
        return (prev, v2_cur), (cur, v2_cur)

    in_specs, args = [], []
    for g in range(DSW_GROUPS):
        for t in range(3):
            prev, cur = specs(g, t)
            for spec, arr in ((prev, cur) if t else (cur,)):
                in_specs.append(spec)
                args.append(arr)
    return pl.pallas_call(
        _attn_kernel,
        grid=(batch, n_tiles, HEAD_PAIRS),
        in_specs=in_specs,
        out_specs=pl.BlockSpec((None, tile, lanes), lambda b, i, j: (b, i, j)),
        out_shape=jax.ShapeDtypeStruct((batch, seq, HEAD_PAIRS * lanes), BF16),
        scratch_shapes=[pltpu.VMEM((tile, lanes), F32)] * (3 * DSW_GROUPS),
        compiler_params=_params("parallel", "parallel", "parallel"),
        name="dilated_attention",
    )(*args)


def _ffn_kernel(*refs, final):
    if final:
        (o_ref, wo_ref, x_ref, mod_ref, gain_ref, wup_ref, cw_ref, cb_ref, wdn_ref, fg_ref,
         out_ref, carry_ref, act_ref) = refs
    else:
        (o_ref, wo_ref, x_ref, mod_ref, gain_ref, wup_ref, cw_ref, cb_ref, wdn_ref,
         out_ref, carry_ref, act_ref) = refs
    tm = x_ref.shape[0]
    tf = FFN_CHUNK

    @pl.when(pl.program_id(1) == 0)
    def _():
        carry_ref[...] = jnp.zeros_like(carry_ref)

    x = x_ref[...] + mod_ref[2:3, :] * _dot(o_ref[...], wo_ref[...])
    h = _modulate(x, gain_ref[...], mod_ref[3:4, :], mod_ref[4:5, :]).astype(BF16)
    row = lax.broadcasted_iota(jnp.int32, (tm, tf), 0)
    for c in range(FFN_DIM // tf):
        cs = slice(c * tf, (c + 1) * tf)
        us = slice(FFN_DIM + c * tf, FFN_DIM + (c + 1) * tf)
        a = _dot(h, wup_ref[:, cs])
        u = _dot(h, wup_ref[:, us])
        prev = carry_ref[:, cs]
        a1 = jnp.where(row == 0, prev[CONV_HALO - 1:CONV_HALO, :], pltpu.roll(a, 1, axis=0))
        a2 = jnp.where(row == 0, prev[CONV_HALO - 2:CONV_HALO - 1, :],
                       jnp.where(row == 1, prev[CONV_HALO - 1:CONV_HALO, :],
                                 pltpu.roll(a, 2, axis=0)))
        carry_ref[:, cs] = a[tm - CONV_HALO:tm, :]
        ac = cw_ref[2:3, cs] * a + cw_ref[1:2, cs] * a1 + cw_ref[0:1, cs] * a2 + cb_ref[:, cs]
        act_ref[:, cs] = (_silu(ac) * u).astype(BF16)
    y = x + mod_ref[5:6, :] * _dot(act_ref[...], wdn_ref[...])
    if final:
        y = _rms(y) * fg_ref[...]
    out_ref[...] = y


def _ffn(o, w_out, x, mod, gain, w_up, conv_w, conv_b, w_down, final_gain=None, tm=512):
    batch, seq, d = x.shape
    f = FFN_DIM
    final = final_gain is not None
    const2 = lambda b, i: (0, 0)
    resident = dict(pipeline_mode=pl.Buffered(1))
    in_specs = [
        pl.BlockSpec((None, tm, d), lambda b, i: (b, i, 0)),
        pl.BlockSpec((d, d), const2, **resident),
        pl.BlockSpec((None, tm, d), lambda b, i: (b, i, 0)),
        pl.BlockSpec((None, 6, d), lambda b, i: (b, 0, 0)),
        pl.BlockSpec((1, d), const2),
        pl.BlockSpec((d, 2 * f), const2, **resident),
        pl.BlockSpec((3, f), const2),
        pl.BlockSpec((1, f), const2),
        pl.BlockSpec((f, d), const2, **resident),
    ]
    args = [o, w_out, x, mod, gain, w_up, conv_w, conv_b, w_down]
    if final:
        in_specs.append(pl.BlockSpec((1, d), const2))
        args.append(final_gain)
    return pl.pallas_call(
        functools.partial(_ffn_kernel, final=final),
        grid=(batch, seq // tm),
        in_specs=in_specs,
        out_specs=pl.BlockSpec((None, tm, d), lambda b, i: (b, i, 0)),
        out_shape=jax.ShapeDtypeStruct((batch, seq, d), F32),
        scratch_shapes=[pltpu.VMEM((CONV_HALO, f), F32), pltpu.VMEM((tm, f), BF16)],
        compiler_params=_params("arbitrary", "arbitrary"),
        name="conv_ffn",
    )(*args)


def _gla_inproj_kernel(x_ref, mod_ref, gain_ref, w_ref, wl_ref, wg_ref, bg_ref,
                       q_ref, k_ref, v_ref, r_ref, la_ref):
    h = _modulate(x_ref[...], gain_ref[...], mod_ref[0:1, :], mod_ref[1:2, :]).astype(BF16)
    q_ref[...] = _dot(h, w_ref[:, 0:GLA_DK])
    k_ref[...] = _dot(h, w_ref[:, GLA_DK:2 * GLA_DK])
    v_ref[...] = _dot(h, w_ref[:, 2 * GLA_DK:2 * GLA_DK + GLA_DV]).astype(BF16)
    r_ref[...] = _dot(h, w_ref[:, 2 * GLA_DK + GLA_DV:])
    g_low = _dot(h, wl_ref[...])
    z = _dot(g_low.astype(BF16), wg_ref[...]) + bg_ref[...]
    log_sig = jnp.minimum(z, 0.0) - jnp.log(1.0 + jnp.exp(-jnp.abs(z)))
    la_ref[...] = log_sig / GLA_TAU


def _gla_inproj(x, mod, gain, w_main, w_low, w_gate, b_gate, tm=512):
    batch, seq, d = x.shape
    const2 = lambda b, i: (0, 0)
    row_spec = lambda n: pl.BlockSpec((None, tm, n), lambda b, i: (b, i, 0))
    shape = lambda n, dt: jax.ShapeDtypeStruct((batch, seq, n), dt)
    return pl.pallas_call(
        _gla_inproj_kernel,
        grid=(batch, seq // tm),
        in_specs=[
            row_spec(d),
            pl.BlockSpec((None, 6, d), lambda b, i: (b, 0, 0)),
            pl.BlockSpec((1, d), const2),
            pl.BlockSpec(w_main.shape, const2),
            pl.BlockSpec(w_low.shape, const2),
            pl.BlockSpec(w_gate.shape, const2),
            pl.BlockSpec((1, GLA_DK), const2),
        ],
        out_specs=[row_spec(GLA_DK), row_spec(GLA_DK), row_spec(GLA_DV), row_spec(GLA_DV),
                   row_spec(GLA_DK)],
        out_shape=[shape(GLA_DK, F32), shape(GLA_DK, F32), shape(GLA_DV, BF16),
                   shape(GLA_DV, F32), shape(GLA_DK, F32)],
        compiler_params=_params("parallel", "parallel"),
        name="gla_inproj",
    )(x, mod, gain, w_main, w_low, w_gate, b_gate)


def _gla_kernel(q_ref, k_ref, v_ref, r_ref, la_ref, gain_ref, tri_ref, o_ref, st_ref):
    @pl.when(pl.program_id(1) == 0)
    def _():
        st_ref[...] = jnp.zeros_like(st_ref)

    ck = GLA_CHUNK
    la = la_ref[...]
    la_hi = la.astype(BF16)
    la_lo = (la - la_hi.astype(F32)).astype(BF16)
    tri = tri_ref[...]
    b = _dot(tri, la_hi) + _dot(tri, la_lo)
    ri = lax.broadcasted_iota(jnp.int32, (ck, ck), 0)
    ci = lax.broadcasted_iota(jnp.int32, (ck, ck), 1)
    causal = ri >= ci
    gain = gain_ref[...]
    scale = GLA_HK ** -0.5

    for c in range(GLA_TILE // ck):
        rs = slice(c * ck, (c + 1) * ck)
        bc = b[rs, :]
        bl = bc[ck - 1:ck, :]
        kc = k_ref[rs, :]
        qd = (q_ref[rs, :] * scale * jnp.exp(bc)).astype(BF16)
        kd = (kc * jnp.exp(-bc)).astype(BF16)
        ke = (kc * jnp.exp(bl - bc)).astype(BF16)
        dec = jnp.exp(bl)
        for hd in range(GLA_HEADS):
            ks = slice(hd * GLA_HK, (hd + 1) * GLA_HK)
            vs = slice(hd * GLA_HV, (hd + 1) * GLA_HV)
            vc = v_ref[rs, vs]
            att = jnp.where(causal, _dot_nt(qd[:, ks], kd[:, ks]), 0.0)
            st = st_ref[hd]
            o = _dot(att.astype(BF16), vc) + _dot_nt(qd[:, ks], st.astype(BF16))
            st_ref[hd] = st * dec[:, ks] + _dot_tn(vc, ke[:, ks])
            rr = r_ref[rs, vs]
            o_ref[rs, vs] = ((_rms(o) * gain) * _silu(rr)).astype(o_ref.dtype)


def _gla(q, k, v, r, log_a, gain):
    batch, seq, _ = q.shape
    t = GLA_TILE
    idx = jnp.arange(t)
    tri = ((idx[:, None] >= idx[None, :]) &
           (idx[:, None] // GLA_CHUNK == idx[None, :] // GLA_CHUNK)).astype(BF16)
    row_spec = lambda n: pl.BlockSpec((None, t, n), lambda b, i: (b, i, 0))
    const2 = lambda b, i: (0, 0)
    return pl.pallas_call(
        _gla_kernel,
        grid=(batch, seq // t),
        in_specs=[row_spec(GLA_DK), row_spec(GLA_DK), row_spec(GLA_DV), row_spec(GLA_DV),
                  row_spec(GLA_DK), pl.BlockSpec((1, GLA_HV), const2), pl.BlockSpec((t, t), const2)],
        out_specs=row_spec(GLA_DV),
        out_shape=jax.ShapeDtypeStruct((batch, seq, GLA_DV), BF16),
        scratch_shapes=[pltpu.VMEM((GLA_HEADS, GLA_HV, GLA_HK), F32)],
        compiler_params=_params("arbitrary", "arbitrary"),
        name="gla_recurrence",
    )(q, k, v, r, log_a, gain, tri)


def kernel(x, c, w_ada, b_ada, norm_mix, norm_ffn, attn_w_in, attn_w_out,
           gla_w_in, gla_w_gate, gla_b_gate, gla_norm, gla_w_out,
           ffn_w_up, ffn_conv_w, ffn_conv_b, ffn_w_down, norm_final):
    batch, seq, d = x.shape
    depth = w_ada.shape[0]
    mods = _ada(c, w_ada, b_ada).reshape(depth, batch, 6, d)

    mod = mods[0]
    qkv = _attn_inproj(x, mod, norm_mix[0].reshape(1, d), attn_w_in[0].astype(BF16))
    o = _attention(qkv)
    x = _ffn(o, attn_w_out[0].astype(BF16), x, mod, norm_ffn[0].reshape(1, d),
             ffn_w_up[0].astype(BF16), ffn_conv_w[0], ffn_conv_b[0].reshape(1, FFN_DIM),
             ffn_w_down[0].astype(BF16))

    mod = mods[1]
    w_in = gla_w_in[0]
    n_main = 2 * GLA_DK + GLA_DV
    w_main = jnp.concatenate([w_in[:, :n_main], w_in[:, n_main + GLA_RANK:]], axis=1).astype(BF16)
    w_low = w_in[:, n_main:n_main + GLA_RANK].astype(BF16)
    q, k, v, r, log_a = _gla_inproj(x, mod, norm_mix[1].reshape(1, d), w_main, w_low,
                                    gla_w_gate[0].astype(BF16), gla_b_gate[0].reshape(1, GLA_DK))
    o = _gla(q, k, v, r, log_a, gla_norm[0].reshape(1, GLA_HV))
    x = _ffn(o, gla_w_out[0].astype(BF16), x, mod, norm_ffn[1].reshape(1, d),
             ffn_w_up[1].astype(BF16), ffn_conv_w[1], ffn_conv_b[1].reshape(1, FFN_DIM),
             ffn_w_down[1].astype(BF16), final_gain=norm_final.reshape(1, d))
    return x
```

```python
import functools
import math

import jax
import jax.numpy as jnp
from jax import lax
from jax.experimental import pallas as pl
from jax.experimental.pallas import tpu as pltpu

F32 = jnp.float32
BF16 = jnp.bfloat16

D_MODEL = 1024
EPS = 1e-6
LANES = 128

DSW_DILATIONS = (1, 4, 16)
DSW_GROUPS = len(DSW_DILATIONS)
DSW_HEADS = 16
DSW_HEAD_DIM = 64
DSW_BLOCK = 128
HEAD_PAIRS = DSW_HEADS // 2
DSW_UNIT = 1024
DSW_TILE = 2048
G0_UNROLL = 5
G2_UNROLL = 4

GLA_HEADS = 4
GLA_DK = 512
GLA_DV = 1024
GLA_HK = 128
GLA_HV = 256
GLA_RANK = 16
GLA_TAU = 16.0
GLA_CHUNK = 64
GLA_TILE = 256

FFN_DIM = 2816
FFN_CHUNK = 256
CONV_HALO = 8

VMEM_LIMIT_BYTES = 56 * 1024 * 1024


def _params(*semantics):
    return pltpu.CompilerParams(dimension_semantics=semantics,
                                vmem_limit_bytes=VMEM_LIMIT_BYTES)


def _rms(x):
    return x * lax.rsqrt(jnp.mean(x * x, axis=-1, keepdims=True) + EPS)


def _modulate(x, gain, shift, scale):
    return (_rms(x) * gain) * (1.0 + scale) + shift


def _silu(x):
    return x * jax.nn.sigmoid(x)


def _dot(a, b):
    return jnp.dot(a, b, preferred_element_type=F32)


def _dot_nt(a, b):
    return lax.dot_general(a, b, (((1,), (1,)), ((), ())), preferred_element_type=F32)


def _dot_tn(a, b):
    return lax.dot_general(a, b, (((0,), (0,)), ((), ())), preferred_element_type=F32)


def _ada_kernel(c_ref, w_ref, b_ref, o_ref):
    cond = _silu(c_ref[...])
    o_ref[...] = _dot(cond, w_ref[...]) + b_ref[...]


def _ada(c, w_ada, b_ada, tn=1536):
    depth, d, n = w_ada.shape
    batch = c.shape[0]
    return pl.pallas_call(
        _ada_kernel,
        grid=(depth, n // tn),
        in_specs=[
            pl.BlockSpec((batch, d), lambda l, j: (0, 0)),
            pl.BlockSpec((None, d, tn), lambda l, j: (l, 0, j)),
            pl.BlockSpec((None, 1, tn), lambda l, j: (l, 0, j)),
        ],
        out_specs=pl.BlockSpec((None, batch, tn), lambda l, j: (l, 0, j)),
        out_shape=jax.ShapeDtypeStruct((depth, batch, n), F32),
        compiler_params=_params("parallel", "parallel"),
        name="ada_mod",
    )(c, w_ada, b_ada.reshape(depth, 1, n))


def _attn_inproj_kernel(x_ref, mod_ref, gain_ref, wq_ref, wk_ref, wv_ref, o_ref, h_ref, lane_ref):
    g_step = pl.program_id(2)
    tm = x_ref.shape[0]

    @pl.when(g_step == 0)
    def _():
        h = _modulate(x_ref[...], gain_ref[...], mod_ref[0:1, :], mod_ref[1:2, :])
        h_ref[0] = h.astype(BF16)
        for c in range(D_MODEL // LANES):
            lane_ref[c] = h[:, c * LANES:(c + 1) * LANES]
        for g in range(1, DSW_GROUPS):
            d = DSW_DILATIONS[g]
            n = tm // d
            for r in range(d):
                for c in range(D_MODEL // LANES):
                    h_ref[g, r * n:(r + 1) * n, c * LANES:(c + 1) * LANES] = (
                        lane_ref[c, pl.ds(r, n, stride=d), :].astype(BF16))

    h = h_ref[g_step]
    q_scale = DSW_HEAD_DIM ** -0.5 * math.log2(math.e)
    for t, w_ref in enumerate((wq_ref, wk_ref, wv_ref)):
        res = _dot(h, w_ref[...])
        if t == 0:
            res = res * q_scale
        for pp in range(HEAD_PAIRS):
            o_ref[t * HEAD_PAIRS + pp] = res[:, pp * LANES:(pp + 1) * LANES].astype(BF16)


def _attn_inproj(x, mod, gain, w):
    batch, seq, d = x.shape
    tm = DSW_UNIT
    n_slots = 3 * DSW_GROUPS

    def w_spec(t):
        return pl.BlockSpec((d, d), lambda b, i, g: (0, t * DSW_GROUPS + g))

    return pl.pallas_call(
        _attn_inproj_kernel,
        grid=(batch, seq // tm, DSW_GROUPS),
        in_specs=[
            pl.BlockSpec((None, tm, d), lambda b, i, g: (b, i, 0)),
            pl.BlockSpec((None, 6, d), lambda b, i, g: (b, 0, 0)),
            pl.BlockSpec((1, d), lambda b, i, g: (0, 0)),
            w_spec(0), w_spec(1), w_spec(2),
        ],
        out_specs=pl.BlockSpec((None, 3 * HEAD_PAIRS, tm, LANES), lambda b, i, g: (b, g, i, 0)),
        out_shape=jax.ShapeDtypeStruct((batch, n_slots * HEAD_PAIRS, seq, LANES), BF16),
        scratch_shapes=[pltpu.VMEM((DSW_GROUPS, tm, d), BF16),
                        pltpu.VMEM((d // LANES, tm, LANES), F32)],
        compiler_params=_params("parallel", "parallel", "arbitrary"),
        name="attn_inproj",
    )(x, mod, gain, w, w, w)


def _attn_kernel(q0, k0p, k0, v0p, v0, q1, k1p, k1, v1p, v1, q2, k2p, k2, v2p, v2,
                 o_ref, pv0, m0, l0, pv1, m1, l1, pv2, m2, l2):
    blk = DSW_BLOCK
    tile = pl.program_id(1)
    row = lax.broadcasted_iota(jnp.int32, (blk, 2 * blk), 0)
    col = lax.broadcasted_iota(jnp.int32, (blk, 2 * blk), 1)
    band = (col >= row) & (col <= row + blk)
    bias_mid = jnp.where(band, 0.0, -jnp.inf).astype(F32)
    lo = jnp.where(tile == 0, blk, 0)
    bias_head = jnp.where(band & (col >= lo), 0.0, -jnp.inf).astype(F32)
    lane = lax.broadcasted_iota(jnp.int32, (blk, LANES), 1)
    left = lane < DSW_HEAD_DIM

    def band_block(q, kk, vv, bias):
        res = []
        for side in range(2):
            keep = left if side == 0 else jnp.logical_not(left)
            qm = jnp.where(keep, q, jnp.zeros_like(q))
            s = _dot_nt(qm, kk) + bias
            m = jnp.max(s, axis=-1, keepdims=True)
            p = jnp.exp2(s - m)
            l = jnp.sum(p, axis=-1, keepdims=True)
            res.append((_dot(p.astype(BF16), vv), m, l))
        return tuple(jnp.where(left, a, b) for a, b in zip(res[0], res[1]))

    def emit(refs, rows, q, kk, vv, bias):
        pv, m, l = band_block(q, kk, vv, bias)
        refs[0][rows, :] = pv
        refs[1][rows, :] = m
        refs[2][rows, :] = l

    cat = lambda parts: jnp.concatenate(parts, axis=0)

    g0 = (pv0, m0, l0)

    def emit0(nb):
        aligned = (lambda v: v) if isinstance(nb, int) else (lambda v: pl.multiple_of(v, blk))
        rows = pl.ds(aligned(nb * blk), blk)
        krows = pl.ds(aligned((nb - 1) * blk), 2 * blk)
        emit(g0, rows, q0[rows, :], k0[krows, :], v0[krows, :], bias_mid)

    emit(g0, pl.ds(0, blk), q0[0:blk, :], cat([k0p[...], k0[0:blk, :]]),
         cat([v0p[...], v0[0:blk, :]]), bias_head)
    for nb in range(1, G0_UNROLL + 1):
        emit0(nb)

    def body0(i, carry):
        for t in range(G0_UNROLL):
            emit0(1 + (i + 1) * G0_UNROLL + t)
        return carry

    lax.fori_loop(0, (DSW_TILE // blk - 1) // G0_UNROLL - 1, body0, 0)

    g1 = (pv1, m1, l1)
    d1 = DSW_DILATIONS[1]

    def body1(r, carry):
        for nb in range(4):
            u, hb = divmod(nb, 2)
            q = q1[u, r, hb * blk:(hb + 1) * blk, :]
            if nb == 0:
                kk = cat([k1p[r], k1[0, r, 0:blk, :]])
                vv = cat([v1p[r], v1[0, r, 0:blk, :]])
            elif nb == 2:
                kk = cat([k1[0, r, blk:2 * blk, :], k1[1, r, 0:blk, :]])
                vv = cat([v1[0, r, blk:2 * blk, :], v1[1, r, 0:blk, :]])
            else:
                kk = k1[u, r, :, :]
                vv = v1[u, r, :, :]
            rows = pl.ds(nb * blk * d1 + r, blk, stride=d1)
            emit(g1, rows, q, kk, vv, bias_head if nb == 0 else bias_mid)
        return carry

    lax.fori_loop(0, d1, body1, 0)

    g2 = (pv2, m2, l2)
    d2 = DSW_DILATIONS[2]

    def body2(i, carry):
        for t in range(G2_UNROLL):
            r = i * G2_UNROLL + t
            q = cat([q2[0, r], q2[1, r]])
            kk = cat([k2p[0, r], k2p[1, r], k2[0, r], k2[1, r]])
            vv = cat([v2p[0, r], v2p[1, r], v2[0, r], v2[1, r]])
            emit(g2, pl.ds(r, blk, stride=d2), q, kk, vv, bias_head)
        return carry

    lax.fori_loop(0, d2 // G2_UNROLL, body2, 0)

    def merge(i, carry):
        rows = pl.ds(pl.multiple_of(i * blk, blk), blk)
        ma, mb, mc = m0[rows, :], m1[rows, :], m2[rows, :]
        mx = jnp.maximum(jnp.maximum(ma, mb), mc)
        ea, eb, ec = jnp.exp2(ma - mx), jnp.exp2(mb - mx), jnp.exp2(mc - mx)
        num = pv0[rows, :] * ea + pv1[rows, :] * eb + pv2[rows, :] * ec
        den = l0[rows, :] * ea + l1[rows, :] * eb + l2[rows, :] * ec
        o_ref[rows, :] = (num / den).astype(o_ref.dtype)
        return carry

    lax.fori_loop(0, DSW_TILE // blk, merge, 0)


def _attention(qkv):
    batch, n_slots, seq, lanes = qkv.shape
    blk, unit, tile = DSW_BLOCK, DSW_UNIT, DSW_TILE
    n_tiles = seq // tile
    upt = tile // unit
    d1, d2 = DSW_DILATIONS[1], DSW_DILATIONS[2]

    def slot(g, t, j):
        return (g * 3 + t) * HEAD_PAIRS + j

    flat_blocks = qkv.reshape(batch, n_slots, seq // blk, blk, lanes)
    v1_cur = qkv.reshape(batch, n_slots, n_tiles, upt, d1, unit // d1, lanes)
    v1_prev = qkv.reshape(batch, n_slots, seq // unit, d1, unit // d1 // blk, blk, lanes)
    v2_cur = qkv.reshape(batch, n_slots, n_tiles, upt, d2, unit // d2, lanes)

    def specs(g, t):
        if g == 0:
            cur = pl.BlockSpec((None, None, tile, lanes), lambda b, i, j: (b, slot(0, t, j), i, 0))
            prev = pl.BlockSpec(
                (None, None, None, blk, lanes),
                lambda b, i, j: (b, slot(0, t, j), jnp.maximum(i * (tile // blk) - 1, 0), 0, 0))
            return (prev, flat_blocks), (cur, qkv)
        if g == 1:
            cur = pl.BlockSpec((None, None, None, upt, d1, unit // d1, lanes),
                               lambda b, i, j: (b, slot(1, t, j), i, 0, 0, 0, 0))
            prev = pl.BlockSpec(
                (None, None, None, d1, None, blk, lanes),
                lambda b, i, j: (b, slot(1, t, j), jnp.maximum(i * upt - 1, 0), 0,
                                 unit // d1 // blk - 1, 0, 0))
            return (prev, v1_prev), (cur, v1_cur)
        cur = pl.BlockSpec((None, None, None, upt, d2, unit // d2, lanes),
                           lambda b, i, j: (b, slot(2, t, j), i, 0, 0, 0, 0))
        prev = pl.BlockSpec((None, None, None, upt, d2, unit // d2, lanes),
                            lambda b, i, j: (b, slot(2, t, j), jnp.maximum(i - 1, 0), 0, 0, 0, 0))
        return (prev, v2_cur), (cur, v2_cur)

    in_specs, args = [], []
    for g in range(DSW_GROUPS):
        for t in range(3):
            prev, cur = specs(g, t)
            for spec, arr in ((prev, cur) if t else (cur,)):
                in_specs.append(spec)
                args.append(arr)
    return pl.pallas_call(
        _attn_kernel,
        grid=(batch, n_tiles, HEAD_PAIRS),
        in_specs=in_specs,
        out_specs=pl.BlockSpec((None, tile, lanes), lambda b, i, j: (b, i, j)),
        out_shape=jax.ShapeDtypeStruct((batch, seq, HEAD_PAIRS * lanes), BF16),
        scratch_shapes=[pltpu.VMEM((tile, lanes), F32)] * (3 * DSW_GROUPS),
        compiler_params=_params("parallel", "parallel", "parallel"),
        name="dilated_attention",
    )(*args)


def _ffn_kernel(*refs, final):
    if final:
        (o_ref, wo_ref, x_ref, mod_ref, gain_ref, wup_ref, cw_ref, cb_ref, wdn_ref, fg_ref,
         out_ref, carry_ref, act_ref) = refs
    else:
        (o_ref, wo_ref, x_ref, mod_ref, gain_ref, wup_ref, cw_ref, cb_ref, wdn_ref,
         out_ref, carry_ref, act_ref) = refs
    tm = x_ref.shape[0]
    tf = FFN_CHUNK

    @pl.when(pl.program_id(1) == 0)
    def _():
        carry_ref[...] = jnp.zeros_like(carry_ref)

    x = x_ref[...] + mod_ref[2:3, :] * _dot(o_ref[...], wo_ref[...])
    h = _modulate(x, gain_ref[...], mod_ref[3:4, :], mod_ref[4:5, :]).astype(BF16)
    row = lax.broadcasted_iota(jnp.int32, (tm, tf), 0)
    for c in range(FFN_DIM // tf):
        cs = slice(c * tf, (c + 1) * tf)
        us = slice(FFN_DIM + c * tf, FFN_DIM + (c + 1) * tf)
        a = _dot(h, wup_ref[:, cs])
        u = _dot(h, wup_ref[:, us])
        prev = carry_ref[:, cs]
        a1 = jnp.where(row == 0, prev[CONV_HALO - 1:CONV_HALO, :], pltpu.roll(a, 1, axis=0))
        a2 = jnp.where(row == 0, prev[CONV_HALO - 2:CONV_HALO - 1, :],
                       jnp.where(row == 1, prev[CONV_HALO - 1:CONV_HALO, :],
                                 pltpu.roll(a, 2, axis=0)))
        carry_ref[:, cs] = a[tm - CONV_HALO:tm, :]
        ac = cw_ref[2:3, cs] * a + cw_ref[1:2, cs] * a1 + cw_ref[0:1, cs] * a2 + cb_ref[:, cs]
        act_ref[:, cs] = (_silu(ac) * u).astype(BF16)
    y = x + mod_ref[5:6, :] * _dot(act_ref[...], wdn_ref[...])
    if final:
        y = _rms(y) * fg_ref[...]
    out_ref[...] = y


def _ffn(o, w_out, x, mod, gain, w_up, conv_w, conv_b, w_down, final_gain=None, tm=512):
    batch, seq, d = x.shape
    f = FFN_DIM
    final = final_gain is not None
    const2 = lambda b, i: (0, 0)
    resident = dict(pipeline_mode=pl.Buffered(1))
    in_specs = [
        pl.BlockSpec((None, tm, d), lambda b, i: (b, i, 0)),
        pl.BlockSpec((d, d), const2, **resident),
        pl.BlockSpec((None, tm, d), lambda b, i: (b, i, 0)),
        pl.BlockSpec((None, 6, d), lambda b, i: (b, 0, 0)),
        pl.BlockSpec((1, d), const2),
        pl.BlockSpec((d, 2 * f), const2, **resident),
        pl.BlockSpec((3, f), const2),
        pl.BlockSpec((1, f), const2),
        pl.BlockSpec((f, d), const2, **resident),
    ]
    args = [o, w_out, x, mod, gain, w_up, conv_w, conv_b, w_down]
    if final:
        in_specs.append(pl.BlockSpec((1, d), const2))
        args.append(final_gain)
    return pl.pallas_call(
        functools.partial(_ffn_kernel, final=final),
        grid=(batch, seq // tm),
        in_specs=in_specs,
        out_specs=pl.BlockSpec((None, tm, d), lambda b, i: (b, i, 0)),
        out_shape=jax.ShapeDtypeStruct((batch, seq, d), F32),
        scratch_shapes=[pltpu.VMEM((CONV_HALO, f), F32), pltpu.VMEM((tm, f), BF16)],
        compiler_params=_params("arbitrary", "arbitrary"),
        name="conv_ffn",
    )(*args)


def _gla_inproj_kernel(x_ref, mod_ref, gain_ref, w_ref, wl_ref, wg_ref, bg_ref,
                       q_ref, k_ref, v_ref, r_ref, la_ref):
    h = _modulate(x_ref[...], gain_ref[...], mod_ref[0:1, :], mod_ref[1:2, :]).astype(BF16)
    q_ref[...] = _dot(h, w_ref[:, 0:GLA_DK])
    k_ref[...] = _dot(h, w_ref[:, GLA_DK:2 * GLA_DK])
    v_ref[...] = _dot(h, w_ref[:, 2 * GLA_DK:2 * GLA_DK + GLA_DV]).astype(BF16)
    r_ref[...] = _dot(h, w_ref[:, 2 * GLA_DK + GLA_DV:])
    g_low = _dot(h, wl_ref[...])
    z = _dot(g_low.astype(BF16), wg_ref[...]) + bg_ref[...]
    log_sig = jnp.minimum(z, 0.0) - jnp.log(1.0 + jnp.exp(-jnp.abs(z)))
    la_ref[...] = log_sig / GLA_TAU


def _gla_inproj(x, mod, gain, w_main, w_low, w_gate, b_gate, tm=512):
    batch, seq, d = x.shape
    const2 = lambda b, i: (0, 0)
    row_spec = lambda n: pl.BlockSpec((None, tm, n), lambda b, i: (b, i, 0))
    shape = lambda n, dt: jax.ShapeDtypeStruct((batch, seq, n), dt)
    return pl.pallas_call(
        _gla_inproj_kernel,
        grid=(batch, seq // tm),
        in_specs=[
            row_spec(d),
            pl.BlockSpec((None, 6, d), lambda b, i: (b, 0, 0)),
            pl.BlockSpec((1, d), const2),
            pl.BlockSpec(w_main.shape, const2),
            pl.BlockSpec(w_low.shape, const2),
            pl.BlockSpec(w_gate.shape, const2),
            pl.BlockSpec((1, GLA_DK), const2),
        ],
        out_specs=[row_spec(GLA_DK), row_spec(GLA_DK), row_spec(GLA_DV), row_spec(GLA_DV),
                   row_spec(GLA_DK)],
        out_shape=[shape(GLA_DK, F32), shape(GLA_DK, F32), shape(GLA_DV, BF16),
                   shape(GLA_DV, F32), shape(GLA_DK, F32)],
        compiler_params=_params("parallel", "parallel"),
        name="gla_inproj",
    )(x, mod, gain, w_main, w_low, w_gate, b_gate)


def _gla_kernel(q_ref, k_ref, v_ref, r_ref, la_ref, gain_ref, tri_ref, o_ref, st_ref):
    @pl.when(pl.program_id(1) == 0)
    def _():
        st_ref[...] = jnp.zeros_like(st_ref)

    ck = GLA_CHUNK
    la = la_ref[...]
    la_hi = la.astype(BF16)
    la_lo = (la - la_hi.astype(F32)).astype(BF16)
    tri = tri_ref[...]
    b = _dot(tri, la_hi) + _dot(tri, la_lo)
    ri = lax.broadcasted_iota(jnp.int32, (ck, ck), 0)
    ci = lax.broadcasted_iota(jnp.int32, (ck, ck), 1)
    causal = ri >= ci
    gain = gain_ref[...]
    scale = GLA_HK ** -0.5

    for c in range(GLA_TILE // ck):
        rs = slice(c * ck, (c + 1) * ck)
        bc = b[rs, :]
        bl = bc[ck - 1:ck, :]
        kc = k_ref[rs, :]
        qd = (q_ref[rs, :] * scale * jnp.exp(bc)).astype(BF16)
        kd = (kc * jnp.exp(-bc)).astype(BF16)
        ke = (kc * jnp.exp(bl - bc)).astype(BF16)
        dec = jnp.exp(bl)
        for hd in range(GLA_HEADS):
            ks = slice(hd * GLA_HK, (hd + 1) * GLA_HK)
            vs = slice(hd * GLA_HV, (hd + 1) * GLA_HV)
            vc = v_ref[rs, vs]
            att = jnp.where(causal, _dot_nt(qd[:, ks], kd[:, ks]), 0.0)
            st = st_ref[hd]
            o = _dot(att.astype(BF16), vc) + _dot_nt(qd[:, ks], st.astype(BF16))
            st_ref[hd] = st * dec[:, ks] + _dot_tn(vc, ke[:, ks])
            rr = r_ref[rs, vs]
            o_ref[rs, vs] = ((_rms(o) * gain) * _silu(rr)).astype(o_ref.dtype)


def _gla(q, k, v, r, log_a, gain):
    batch, seq, _ = q.shape
    t = GLA_TILE
    idx = jnp.arange(t)
    tri = ((idx[:, None] >= idx[None, :]) &
           (idx[:, None] // GLA_CHUNK == idx[None, :] // GLA_CHUNK)).astype(BF16)
    row_spec = lambda n: pl.BlockSpec((None, t, n), lambda b, i: (b, i, 0))
    const2 = lambda b, i: (0, 0)
    return pl.pallas_call(
        _gla_kernel,
        grid=(batch, seq // t),
        in_specs=[row_spec(GLA_DK), row_spec(GLA_DK), row_spec(GLA_DV), row_spec(GLA_DV),
                  row_spec(GLA_DK), pl.BlockSpec((1, GLA_HV), const2), pl.BlockSpec((t, t), const2)],
        out_specs=row_spec(GLA_DV),
        out_shape=jax.ShapeDtypeStruct((batch, seq, GLA_DV), BF16),
        scratch_shapes=[pltpu.VMEM((GLA_HEADS, GLA_HV, GLA_HK), F32)],
        compiler_params=_params("arbitrary", "arbitrary"),
        name="gla_recurrence",
    )(q, k, v, r, log_a, gain, tri)


def kernel(x, c, w_ada, b_ada, norm_mix, norm_ffn, attn_w_in, attn_w_out,
           gla_w_in, gla_w_gate, gla_b_gate, gla_norm, gla_w_out,
           ffn_w_up, ffn_conv_w, ffn_conv_b, ffn_w_down, norm_final):
    batch, seq, d = x.shape
    depth = w_ada.shape[0]
    mods = _ada(c, w_ada, b_ada).reshape(depth, batch, 6, d)

    mod = mods[0]
    qkv = _attn_inproj(x, mod, norm_mix[0].reshape(1, d), attn_w_in[0].astype(BF16))
    o = _attention(qkv)
    x = _ffn(o, attn_w_out[0].astype(BF16), x, mod, norm_ffn[0].reshape(1, d),
             ffn_w_up[0].astype(BF16), ffn_conv_w[0], ffn_conv_b[0].reshape(1, FFN_DIM),
             ffn_w_down[0].astype(BF16))

    mod = mods[1]
    w_in = gla_w_in[0]
    n_main = 2 * GLA_DK + GLA_DV
    w_main = jnp.concatenate([w_in[:, :n_main], w_in[:, n_main + GLA_RANK:]], axis=1).astype(BF16)
    w_low = w_in[:, n_main:n_main + GLA_RANK].astype(BF16)
    q, k, v, r, log_a = _gla_inproj(x, mod, norm_mix[1].reshape(1, d), w_main, w_low,
                                    gla_w_gate[0].astype(BF16), gla_b_gate[0].reshape(1, GLA_DK))
    o = _gla(q, k, v, r, log_a, gla_norm[0].reshape(1, GLA_HV))
    x = _ffn(o, gla_w_out[0].astype(BF16), x, mod, norm_ffn[1].reshape(1, d),
             ffn_w_up[1].astype(BF16), ffn_conv_w[1], ffn_conv_b[1].reshape(1, FFN_DIM),
             ffn_w_down[1].astype(BF16), final_gain=norm_final.reshape(1, d))
    return x
```

```python
import functools
import math

import jax
import jax.numpy as jnp
from jax import lax
from jax.experimental import pallas as pl
from jax.experimental.pallas import tpu as pltpu

F32 = jnp.float32
BF16 = jnp.bfloat16

D_MODEL = 1024
EPS = 1e-6
LANES = 128

DSW_DILATIONS = (1, 4, 16)
DSW_GROUPS = len(DSW_DILATIONS)
DSW_HEADS = 16
DSW_HEAD_DIM = 64
DSW_BLOCK = 128
HEAD_PAIRS = DSW_HEADS // 2
DSW_UNIT = 1024
DSW_TILE = 2048
G0_UNROLL = 15
G1_UNROLL = 2
G2_UNROLL = 8

GLA_HEADS = 4
GLA_DK = 512
GLA_DV = 1024
GLA_HK = 128
GLA_HV = 256
GLA_RANK = 16
GLA_TAU = 16.0
GLA_CHUNK = 64
GLA_TILE = 256

FFN_DIM = 2816
FFN_CHUNK = 256
CONV_HALO = 8

VMEM_LIMIT_BYTES = 56 * 1024 * 1024


def _params(*semantics):
    return pltpu.CompilerParams(dimension_semantics=semantics,
                                vmem_limit_bytes=VMEM_LIMIT_BYTES)


def _rms(x):
    return x * lax.rsqrt(jnp.mean(x * x, axis=-1, keepdims=True) + EPS)


def _modulate(x, gain, shift, scale):
    return (_rms(x) * gain) * (1.0 + scale) + shift


def _silu(x):
    return x * jax.nn.sigmoid(x)


def _dot(a, b):
    return jnp.dot(a, b, preferred_element_type=F32)


def _dot_nt(a, b):
    return lax.dot_general(a, b, (((1,), (1,)), ((), ())), preferred_element_type=F32)


def _dot_tn(a, b):
    return lax.dot_general(a, b, (((0,), (0,)), ((), ())), preferred_element_type=F32)


def _ada_kernel(c_ref, w_ref, b_ref, o_ref):
    cond = _silu(c_ref[...])
    o_ref[...] = _dot(cond, w_ref[...]) + b_ref[...]


def _ada(c, w_ada, b_ada, tn=1536):
    depth, d, n = w_ada.shape
    batch = c.shape[0]
    return pl.pallas_call(
        _ada_kernel,
        grid=(depth, n // tn),
        in_specs=[
            pl.BlockSpec((batch, d), lambda l, j: (0, 0)),
            pl.BlockSpec((None, d, tn), lambda l, j: (l, 0, j)),
            pl.BlockSpec((None, 1, tn), lambda l, j: (l, 0, j)),
        ],
        out_specs=pl.BlockSpec((None, batch, tn), lambda l, j: (l, 0, j)),
        out_shape=jax.ShapeDtypeStruct((depth, batch, n), F32),
        compiler_params=_params("parallel", "parallel"),
        name="ada_mod",
    )(c, w_ada, b_ada.reshape(depth, 1, n))


def _attn_inproj_kernel(x_ref, mod_ref, gain_ref, wq_ref, wk_ref, wv_ref, o_ref, h_ref, lane_ref):
    g_step = pl.program_id(2)
    tm = x_ref.shape[0]

    @pl.when(g_step == 0)
    def _():
        h = _modulate(x_ref[...], gain_ref[...], mod_ref[0:1, :], mod_ref[1:2, :])
        h_ref[0] = h.astype(BF16)
        for c in range(D_MODEL // LANES):
            lane_ref[c] = h[:, c * LANES:(c + 1) * LANES]
        for g in range(1, DSW_GROUPS):
            d = DSW_DILATIONS[g]
            n = tm // d
            for r in range(d):
                for c in range(D_MODEL // LANES):
                    h_ref[g, r * n:(r + 1) * n, c * LANES:(c + 1) * LANES] = (
                        lane_ref[c, pl.ds(r, n, stride=d), :].astype(BF16))

    h = h_ref[g_step]
    q_scale = DSW_HEAD_DIM ** -0.5 * math.log2(math.e)
    for t, w_ref in enumerate((wq_ref, wk_ref, wv_ref)):
        res = _dot(h, w_ref[...])
        if t == 0:
            res = res * q_scale
        for pp in range(HEAD_PAIRS):
            o_ref[t * HEAD_PAIRS + pp] = res[:, pp * LANES:(pp + 1) * LANES].astype(BF16)


def _attn_inproj(x, mod, gain, w):
    batch, seq, d = x.shape
    tm = DSW_UNIT
    n_slots = 3 * DSW_GROUPS

    def w_spec(t):
        return pl.BlockSpec((d, d), lambda b, i, g: (0, t * DSW_GROUPS + g))

    return pl.pallas_call(
        _attn_inproj_kernel,
        grid=(batch, seq // tm, DSW_GROUPS),
        in_specs=[
            pl.BlockSpec((None, tm, d), lambda b, i, g: (b, i, 0)),
            pl.BlockSpec((None, 6, d), lambda b, i, g: (b, 0, 0)),
            pl.BlockSpec((1, d), lambda b, i, g: (0, 0)),
            w_spec(0), w_spec(1), w_spec(2),
        ],
        out_specs=pl.BlockSpec((None, 3 * HEAD_PAIRS, tm, LANES), lambda b, i, g: (b, g, i, 0)),
        out_shape=jax.ShapeDtypeStruct((batch, n_slots * HEAD_PAIRS, seq, LANES), BF16),
        scratch_shapes=[pltpu.VMEM((DSW_GROUPS, tm, d), BF16),
                        pltpu.VMEM((d // LANES, tm, LANES), F32)],
        compiler_params=_params("parallel", "parallel", "arbitrary"),
        name="attn_inproj",
    )(x, mod, gain, w, w, w)


def _attn_kernel(q0, k0p, k0, v0p, v0, q1, k1p, k1, v1p, v1, q2, k2p, k2, v2p, v2,
                 o_ref, pv0, m0, l0, pv1, m1, l1, pv2, m2, l2):
    blk = DSW_BLOCK
    tile = pl.program_id(1)
    row = lax.broadcasted_iota(jnp.int32, (blk, 2 * blk), 0)
    col = lax.broadcasted_iota(jnp.int32, (blk, 2 * blk), 1)
    band = (col >= row) & (col <= row + blk)
    bias_mid = jnp.where(band, 0.0, -jnp.inf).astype(F32)
    lo = jnp.where(tile == 0, blk, 0)
    bias_head = jnp.where(band & (col >= lo), 0.0, -jnp.inf).astype(F32)
    lane = lax.broadcasted_iota(jnp.int32, (blk, LANES), 1)
    left = lane < DSW_HEAD_DIM

    def band_block(q, kk, vv, bias):
        res = []
        for side in range(2):
            keep = left if side == 0 else jnp.logical_not(left)
            qm = jnp.where(keep, q, jnp.zeros_like(q))
            s = _dot_nt(qm, kk) + bias
            m = jnp.max(s, axis=-1, keepdims=True)
            p = jnp.exp2(s - m)
            l = jnp.sum(p, axis=-1, keepdims=True)
            res.append((_dot(p.astype(BF16), vv), m, l))
        return tuple(jnp.where(left, a, b) for a, b in zip(res[0], res[1]))

    def emit(refs, rows, q, kk, vv, bias):
        pv, m, l = band_block(q, kk, vv, bias)
        refs[0][rows, :] = pv
        refs[1][rows, :] = m
        refs[2][rows, :] = l

    cat = lambda parts: jnp.concatenate(parts, axis=0)

    g0 = (pv0, m0, l0)

    def emit0(nb):
        aligned = (lambda v: v) if isinstance(nb, int) else (lambda v: pl.multiple_of(v, blk))
        rows = pl.ds(aligned(nb * blk), blk)
        krows = pl.ds(aligned((nb - 1) * blk), 2 * blk)
        emit(g0, rows, q0[rows, :], k0[krows, :], v0[krows, :], bias_mid)

    emit(g0, pl.ds(0, blk), q0[0:blk, :], cat([k0p[...], k0[0:blk, :]]),
         cat([v0p[...], v0[0:blk, :]]), bias_head)
    for nb in range(1, G0_UNROLL + 1):
        emit0(nb)

    def body0(i, carry):
        for t in range(G0_UNROLL):
            emit0(1 + (i + 1) * G0_UNROLL + t)
        return carry

    lax.fori_loop(0, (DSW_TILE // blk - 1) // G0_UNROLL - 1, body0, 0)

    g1 = (pv1, m1, l1)
    d1 = DSW_DILATIONS[1]

    def body1(i, carry):
        for t in range(G1_UNROLL):
            residue_blocks(i * G1_UNROLL + t)
        return carry

    def residue_blocks(r):
        for nb in range(4):
            u, hb = divmod(nb, 2)
            q = q1[u, r, hb * blk:(hb + 1) * blk, :]
            if nb == 0:
                kk = cat([k1p[r], k1[0, r, 0:blk, :]])
                vv = cat([v1p[r], v1[0, r, 0:blk, :]])
            elif nb == 2:
                kk = cat([k1[0, r, blk:2 * blk, :], k1[1, r, 0:blk, :]])
                vv = cat([v1[0, r, blk:2 * blk, :], v1[1, r, 0:blk, :]])
            else:
                kk = k1[u, r, :, :]
                vv = v1[u, r, :, :]
            rows = pl.ds(nb * blk * d1 + r, blk, stride=d1)
            emit(g1, rows, q, kk, vv, bias_head if nb == 0 else bias_mid)

    lax.fori_loop(0, d1 // G1_UNROLL, body1, 0)

    g2 = (pv2, m2, l2)
    d2 = DSW_DILATIONS[2]

    def body2(i, carry):
        for t in range(G2_UNROLL):
            r = i * G2_UNROLL + t
            q = cat([q2[0, r], q2[1, r]])
            kk = cat([k2p[0, r], k2p[1, r], k2[0, r], k2[1, r]])
            vv = cat([v2p[0, r], v2p[1, r], v2[0, r], v2[1, r]])
            emit(g2, pl.ds(r, blk, stride=d2), q, kk, vv, bias_head)
        return carry

    lax.fori_loop(0, d2 // G2_UNROLL, body2, 0)

    def merge(i, carry):
        rows = pl.ds(pl.multiple_of(i * blk, blk), blk)
        ma, mb, mc = m0[rows, :], m1[rows, :], m2[rows, :]
        mx = jnp.maximum(jnp.maximum(ma, mb), mc)
        ea, eb, ec = jnp.exp2(ma - mx), jnp.exp2(mb - mx), jnp.exp2(mc - mx)
        num = pv0[rows, :] * ea + pv1[rows, :] * eb + pv2[rows, :] * ec
        den = l0[rows, :] * ea + l1[rows, :] * eb + l2[rows, :] * ec
        o_ref[rows, :] = (num / den).astype(o_ref.dtype)
        return carry

    lax.fori_loop(0, DSW_TILE // blk, merge, 0)


def _attention(qkv):
    batch, n_slots, seq, lanes = qkv.shape
    blk, unit, tile = DSW_BLOCK, DSW_UNIT, DSW_TILE
    n_tiles = seq // tile
    upt = tile // unit
    d1, d2 = DSW_DILATIONS[1], DSW_DILATIONS[2]

    def slot(g, t, j):
        return (g * 3 + t) * HEAD_PAIRS + j

    flat_blocks = qkv.reshape(batch, n_slots, seq // blk, blk, lanes)
    v1_cur = qkv.reshape(batch, n_slots, n_tiles, upt, d1, unit // d1, lanes)
    v1_prev = qkv.reshape(batch, n_slots, seq // unit, d1, unit // d1 // blk, blk, lanes)
    v2_cur = qkv.reshape(batch, n_slots, n_tiles, upt, d2, unit // d2, lanes)

    def specs(g, t):
        if g == 0:
            cur = pl.BlockSpec((None, None, tile, lanes), lambda b, i, j: (b, slot(0, t, j), i, 0))
            prev = pl.BlockSpec(
                (None, None, None, blk, lanes),
                lambda b, i, j: (b, slot(0, t, j), jnp.maximum(i * (tile // blk) - 1, 0), 0, 0))
            return (prev, flat_blocks), (cur, qkv)
        if g == 1:
            cur = pl.BlockSpec((None, None, None, upt, d1, unit // d1, lanes),
                               lambda b, i, j: (b, slot(1, t, j), i, 0, 0, 0, 0))
            prev = pl.BlockSpec(
                (None, None, None, d1, None, blk, lanes),
                lambda b, i, j: (b, slot(1, t, j), jnp.maximum(i * upt - 1, 0), 0,
                                 unit // d1 // blk - 1, 0, 0))
            return (prev, v1_prev), (cur, v1_cur)
        cur = pl.BlockSpec((None, None, None, upt, d2, unit // d2, lanes),
                           lambda b, i, j: (b, slot(2, t, j), i, 0, 0, 0, 0))
        prev = pl.BlockSpec((None, None, None, upt, d2, unit // d2, lanes),
                            lambda b, i, j: (b, slot(2, t, j), jnp.maximum(i - 1, 0), 0, 0, 0, 0))
        return (prev, v2_cur), (cur, v2_cur)

    in_specs, args = [], []
    for g in range(DSW_GROUPS):
        for t in range(3):
            prev, cur = specs(g, t)
            for spec, arr in ((prev, cur) if t else (cur,)):
                in_specs.append(spec)
                args.append(arr)
    return pl.pallas_call(
        _attn_kernel,
        grid=(batch, n_tiles, HEAD_PAIRS),
        in_specs=in_specs,
        out_specs=pl.BlockSpec((None, tile, lanes), lambda b, i, j: (b, i, j)),
        out_shape=jax.ShapeDtypeStruct((batch, seq, HEAD_PAIRS * lanes), BF16),
        scratch_shapes=[pltpu.VMEM((tile, lanes), F32)] * (3 * DSW_GROUPS),
        compiler_params=_params("parallel", "parallel", "parallel"),
        name="dilated_attention",
    )(*args)


def _ffn_kernel(*refs, final):
    if final:
        (o_ref, wo_ref, x_ref, mod_ref, gain_ref, wup_ref, cw_ref, cb_ref, wdn_ref, fg_ref,
         out_ref, carry_ref, act_ref) = refs
    else:
        (o_ref, wo_ref, x_ref, mod_ref, gain_ref, wup_ref, cw_ref, cb_ref, wdn_ref,
         out_ref, carry_ref, act_ref) = refs
    tm = x_ref.shape[0]
    tf = FFN_CHUNK

    @pl.when(pl.program_id(1) == 0)
    def _():
        carry_ref[...] = jnp.zeros_like(carry_ref)

    x = x_ref[...] + mod_ref[2:3, :] * _dot(o_ref[...], wo_ref[...])
    h = _modulate(x, gain_ref[...], mod_ref[3:4, :], mod_ref[4:5, :]).astype(BF16)
    row = lax.broadcasted_iota(jnp.int32, (tm, tf), 0)
    for c in range(FFN_DIM // tf):
        cs = slice(c * tf, (c + 1) * tf)
        us = slice(FFN_DIM + c * tf, FFN_DIM + (c + 1) * tf)
        a = _dot(h, wup_ref[:, cs])
        u = _dot(h, wup_ref[:, us])
        prev = carry_ref[:, cs]
        a1 = jnp.where(row == 0, prev[CONV_HALO - 1:CONV_HALO, :], pltpu.roll(a, 1, axis=0))
        a2 = jnp.where(row == 0, prev[CONV_HALO - 2:CONV_HALO - 1, :],
                       jnp.where(row == 1, prev[CONV_HALO - 1:CONV_HALO, :],
                                 pltpu.roll(a, 2, axis=0)))
        carry_ref[:, cs] = a[tm - CONV_HALO:tm, :]
        ac = cw_ref[2:3, cs] * a + cw_ref[1:2, cs] * a1 + cw_ref[0:1, cs] * a2 + cb_ref[:, cs]
        act_ref[:, cs] = (_silu(ac) * u).astype(BF16)
    y = x + mod_ref[5:6, :] * _dot(act_ref[...], wdn_ref[...])
    if final:
        y = _rms(y) * fg_ref[...]
    out_ref[...] = y


def _ffn(o, w_out, x, mod, gain, w_up, conv_w, conv_b, w_down, final_gain=None, tm=512):
    batch, seq, d = x.shape
    f = FFN_DIM
    final = final_gain is not None
    const2 = lambda b, i: (0, 0)
    resident = dict(pipeline_mode=pl.Buffered(1))
    in_specs = [
        pl.BlockSpec((None, tm, d), lambda b, i: (b, i, 0)),
        pl.BlockSpec((d, d), const2, **resident),
        pl.BlockSpec((None, tm, d), lambda b, i: (b, i, 0)),
        pl.BlockSpec((None, 6, d), lambda b, i: (b, 0, 0)),
        pl.BlockSpec((1, d), const2),
        pl.BlockSpec((d, 2 * f), const2, **resident),
        pl.BlockSpec((3, f), const2),
        pl.BlockSpec((1, f), const2),
        pl.BlockSpec((f, d), const2, **resident),
    ]
    args = [o, w_out, x, mod, gain, w_up, conv_w, conv_b, w_down]
    if final:
        in_specs.append(pl.BlockSpec((1, d), const2))
        args.append(final_gain)
    return pl.pallas_call(
        functools.partial(_ffn_kernel, final=final),
        grid=(batch, seq // tm),
        in_specs=in_specs,
        out_specs=pl.BlockSpec((None, tm, d), lambda b, i: (b, i, 0)),
        out_shape=jax.ShapeDtypeStruct((batch, seq, d), F32),
        scratch_shapes=[pltpu.VMEM((CONV_HALO, f), F32), pltpu.VMEM((tm, f), BF16)],
        compiler_params=_params("arbitrary", "arbitrary"),
        name="conv_ffn",
    )(*args)


def _gla_inproj_kernel(x_ref, mod_ref, gain_ref, w_ref, wl_ref, wg_ref, bg_ref,
                       q_ref, k_ref, v_ref, r_ref, la_ref):
    h = _modulate(x_ref[...], gain_ref[...], mod_ref[0:1, :], mod_ref[1:2, :]).astype(BF16)
    q_ref[...] = _dot(h, w_ref[:, 0:GLA_DK])
    k_ref[...] = _dot(h, w_ref[:, GLA_DK:2 * GLA_DK])
    v_ref[...] = _dot(h, w_ref[:, 2 * GLA_DK:2 * GLA_DK + GLA_DV]).astype(BF16)
    r_ref[...] = _dot(h, w_ref[:, 2 * GLA_DK + GLA_DV:])
    g_low = _dot(h, wl_ref[...])
    z = _dot(g_low.astype(BF16), wg_ref[...]) + bg_ref[...]
    log_sig = jnp.minimum(z, 0.0) - jnp.log(1.0 + jnp.exp(-jnp.abs(z)))
    la_ref[...] = log_sig / GLA_TAU


def _gla_inproj(x, mod, gain, w_main, w_low, w_gate, b_gate, tm=512):
    batch, seq, d = x.shape
    const2 = lambda b, i: (0, 0)
    row_spec = lambda n: pl.BlockSpec((None, tm, n), lambda b, i: (b, i, 0))
    shape = lambda n, dt: jax.ShapeDtypeStruct((batch, seq, n), dt)
    return pl.pallas_call(
        _gla_inproj_kernel,
        grid=(batch, seq // tm),
        in_specs=[
            row_spec(d),
            pl.BlockSpec((None, 6, d), lambda b, i: (b, 0, 0)),
            pl.BlockSpec((1, d), const2),
            pl.BlockSpec(w_main.shape, const2),
            pl.BlockSpec(w_low.shape, const2),
            pl.BlockSpec(w_gate.shape, const2),
            pl.BlockSpec((1, GLA_DK), const2),
        ],
        out_specs=[row_spec(GLA_DK), row_spec(GLA_DK), row_spec(GLA_DV), row_spec(GLA_DV),
                   row_spec(GLA_DK)],
        out_shape=[shape(GLA_DK, F32), shape(GLA_DK, F32), shape(GLA_DV, BF16),
                   shape(GLA_DV, F32), shape(GLA_DK, F32)],
        compiler_params=_params("parallel", "parallel"),
        name="gla_inproj",
    )(x, mod, gain, w_main, w_low, w_gate, b_gate)


def _gla_kernel(q_ref, k_ref, v_ref, r_ref, la_ref, gain_ref, tri_ref, o_ref, st_ref):
    @pl.when(pl.program_id(1) == 0)
    def _():
        st_ref[...] = jnp.zeros_like(st_ref)

    ck = GLA_CHUNK
    la = la_ref[...]
    la_hi = la.astype(BF16)
    la_lo = (la - la_hi.astype(F32)).astype(BF16)
    tri = tri_ref[...]
    b = _dot(tri, la_hi) + _dot(tri, la_lo)
    ri = lax.broadcasted_iota(jnp.int32, (ck, ck), 0)
    ci = lax.broadcasted_iota(jnp.int32, (ck, ck), 1)
    causal = ri >= ci
    gain = gain_ref[...]
    scale = GLA_HK ** -0.5

    for c in range(GLA_TILE // ck):
        rs = slice(c * ck, (c + 1) * ck)
        bc = b[rs, :]
        bl = bc[ck - 1:ck, :]
        kc = k_ref[rs, :]
        qd = (q_ref[rs, :] * scale * jnp.exp(bc)).astype(BF16)
        kd = (kc * jnp.exp(-bc)).astype(BF16)
        ke = (kc * jnp.exp(bl - bc)).astype(BF16)
        dec = jnp.exp(bl)
        for hd in range(GLA_HEADS):
            ks = slice(hd * GLA_HK, (hd + 1) * GLA_HK)
            vs = slice(hd * GLA_HV, (hd + 1) * GLA_HV)
            vc = v_ref[rs, vs]
            att = jnp.where(causal, _dot_nt(qd[:, ks], kd[:, ks]), 0.0)
            st = st_ref[hd]
            o = _dot(att.astype(BF16), vc) + _dot_nt(qd[:, ks], st.astype(BF16))
            st_ref[hd] = st * dec[:, ks] + _dot_tn(vc, ke[:, ks])
            rr = r_ref[rs, vs]
            o_ref[rs, vs] = ((_rms(o) * gain) * _silu(rr)).astype(o_ref.dtype)


def _gla(q, k, v, r, log_a, gain):
    batch, seq, _ = q.shape
    t = GLA_TILE
    idx = jnp.arange(t)
    tri = ((idx[:, None] >= idx[None, :]) &
           (idx[:, None] // GLA_CHUNK == idx[None, :] // GLA_CHUNK)).astype(BF16)
    row_spec = lambda n: pl.BlockSpec((None, t, n), lambda b, i: (b, i, 0))
    const2 = lambda b, i: (0, 0)
    return pl.pallas_call(
        _gla_kernel,
        grid=(batch, seq // t),
        in_specs=[row_spec(GLA_DK), row_spec(GLA_DK), row_spec(GLA_DV), row_spec(GLA_DV),
                  row_spec(GLA_DK), pl.BlockSpec((1, GLA_HV), const2), pl.BlockSpec((t, t), const2)],
        out_specs=row_spec(GLA_DV),
        out_shape=jax.ShapeDtypeStruct((batch, seq, GLA_DV), BF16),
        scratch_shapes=[pltpu.VMEM((GLA_HEADS, GLA_HV, GLA_HK), F32)],
        compiler_params=_params("arbitrary", "arbitrary"),
        name="gla_recurrence",
    )(q, k, v, r, log_a, gain, tri)


def kernel(x, c, w_ada, b_ada, norm_mix, norm_ffn, attn_w_in, attn_w_out,
           gla_w_in, gla_w_gate, gla_b_gate, gla_norm, gla_w_out,
           ffn_w_up, ffn_conv_w, ffn_conv_b, ffn_w_down, norm_final):
    batch, seq, d = x.shape
    depth = w_ada.shape[0]
    mods = _ada(c, w_ada, b_ada).reshape(depth, batch, 6, d)

    mod = mods[0]
    qkv = _attn_inproj(x, mod, norm_mix[0].reshape(1, d), attn_w_in[0].astype(BF16))
    o = _attention(qkv)
    x = _ffn(o, attn_w_out[0].astype(BF16), x, mod, norm_ffn[0].reshape(1, d),
             ffn_w_up[0].astype(BF16), ffn_conv_w[0], ffn_conv_b[0].reshape(1, FFN_DIM),
             ffn_w_down[0].astype(BF16))

    mod = mods[1]
    w_in = gla_w_in[0]
    n_main = 2 * GLA_DK + GLA_DV
    w_main = jnp.concatenate([w_in[:, :n_main], w_in[:, n_main + GLA_RANK:]], axis=1).astype(BF16)
    w_low = w_in[:, n_main:n_main + GLA_RANK].astype(BF16)
    q, k, v, r, log_a = _gla_inproj(x, mod, norm_mix[1].reshape(1, d), w_main, w_low,
                                    gla_w_gate[0].astype(BF16), gla_b_gate[0].reshape(1, GLA_DK))
    o = _gla(q, k, v, r, log_a, gla_norm[0].reshape(1, GLA_HV))
    x = _ffn(o, gla_w_out[0].astype(BF16), x, mod, norm_ffn[1].reshape(1, d),
             ffn_w_up[1].astype(BF16), ffn_conv_w[1], ffn_conv_b[1].reshape(1, FFN_DIM),
             ffn_w_down[1].astype(BF16), final_gain=norm_final.reshape(1, d))
    return x
```

```python
import functools
import math

import jax
import jax.numpy as jnp
from jax import lax
from jax.experimental import pallas as pl
from jax.experimental.pallas import tpu as pltpu

F32 = jnp.float32
BF16 = jnp.bfloat16

D_MODEL = 1024
EPS = 1e-6
LANES = 128

DSW_DILATIONS = (1, 4, 16)
DSW_GROUPS = len(DSW_DILATIONS)
DSW_HEADS = 16
DSW_HEAD_DIM = 64
DSW_BLOCK = 128
HEAD_PAIRS = DSW_HEADS // 2
DSW_UNIT = 1024
DSW_TILE = 2048
G0_UNROLL = 15
G1_UNROLL = 4
G2_UNROLL = 16

GLA_HEADS = 4
GLA_DK = 512
GLA_DV = 1024
GLA_HK = 128
GLA_HV = 256
GLA_RANK = 16
GLA_TAU = 16.0
GLA_CHUNK = 64
GLA_TILE = 256

FFN_DIM = 2816
FFN_CHUNK = 256
CONV_HALO = 8

VMEM_LIMIT_BYTES = 56 * 1024 * 1024


def _params(*semantics):
    return pltpu.CompilerParams(dimension_semantics=semantics,
                                vmem_limit_bytes=VMEM_LIMIT_BYTES)


def _rms(x):
    return x * lax.rsqrt(jnp.mean(x * x, axis=-1, keepdims=True) + EPS)


def _modulate(x, gain, shift, scale):
    return (_rms(x) * gain) * (1.0 + scale) + shift


def _silu(x):
    return x * jax.nn.sigmoid(x)


def _dot(a, b):
    return jnp.dot(a, b, preferred_element_type=F32)


def _dot_nt(a, b):
    return lax.dot_general(a, b, (((1,), (1,)), ((), ())), preferred_element_type=F32)


def _dot_tn(a, b):
    return lax.dot_general(a, b, (((0,), (0,)), ((), ())), preferred_element_type=F32)


def _ada_kernel(c_ref, w_ref, b_ref, o_ref):
    cond = _silu(c_ref[...])
    o_ref[...] = _dot(cond, w_ref[...]) + b_ref[...]


def _ada(c, w_ada, b_ada, tn=1536):
    depth, d, n = w_ada.shape
    batch = c.shape[0]
    return pl.pallas_call(
        _ada_kernel,
        grid=(depth, n // tn),
        in_specs=[
            pl.BlockSpec((batch, d), lambda l, j: (0, 0)),
            pl.BlockSpec((None, d, tn), lambda l, j: (l, 0, j)),
            pl.BlockSpec((None, 1, tn), lambda l, j: (l, 0, j)),
        ],
        out_specs=pl.BlockSpec((None, batch, tn), lambda l, j: (l, 0, j)),
        out_shape=jax.ShapeDtypeStruct((depth, batch, n), F32),
        compiler_params=_params("parallel", "parallel"),
        name="ada_mod",
    )(c, w_ada, b_ada.reshape(depth, 1, n))


def _attn_inproj_kernel(x_ref, mod_ref, gain_ref, wq_ref, wk_ref, wv_ref, o_ref, h_ref, lane_ref):
    g_step = pl.program_id(2)
    tm = x_ref.shape[0]

    @pl.when(g_step == 0)
    def _():
        h = _modulate(x_ref[...], gain_ref[...], mod_ref[0:1, :], mod_ref[1:2, :])
        h_ref[0] = h.astype(BF16)
        for c in range(D_MODEL // LANES):
            lane_ref[c] = h[:, c * LANES:(c + 1) * LANES]
        for g in range(1, DSW_GROUPS):
            d = DSW_DILATIONS[g]
            n = tm // d
            for r in range(d):
                for c in range(D_MODEL // LANES):
                    h_ref[g, r * n:(r + 1) * n, c * LANES:(c + 1) * LANES] = (
                        lane_ref[c, pl.ds(r, n, stride=d), :].astype(BF16))

    h = h_ref[g_step]
    q_scale = DSW_HEAD_DIM ** -0.5 * math.log2(math.e)
    for t, w_ref in enumerate((wq_ref, wk_ref, wv_ref)):
        res = _dot(h, w_ref[...])
        if t == 0:
            res = res * q_scale
        for pp in range(HEAD_PAIRS):
            o_ref[t * HEAD_PAIRS + pp] = res[:, pp * LANES:(pp + 1) * LANES].astype(BF16)


def _attn_inproj(x, mod, gain, w):
    batch, seq, d = x.shape
    tm = DSW_UNIT
    n_slots = 3 * DSW_GROUPS

    def w_spec(t):
        return pl.BlockSpec((d, d), lambda b, i, g: (0, t * DSW_GROUPS + g))

    return pl.pallas_call(
        _attn_inproj_kernel,
        grid=(batch, seq // tm, DSW_GROUPS),
        in_specs=[
            pl.BlockSpec((None, tm, d), lambda b, i, g: (b, i, 0)),
            pl.BlockSpec((None, 6, d), lambda b, i, g: (b, 0, 0)),
            pl.BlockSpec((1, d), lambda b, i, g: (0, 0)),
            w_spec(0), w_spec(1), w_spec(2),
        ],
        out_specs=pl.BlockSpec((None, 3 * HEAD_PAIRS, tm, LANES), lambda b, i, g: (b, g, i, 0)),
        out_shape=jax.ShapeDtypeStruct((batch, n_slots * HEAD_PAIRS, seq, LANES), BF16),
        scratch_shapes=[pltpu.VMEM((DSW_GROUPS, tm, d), BF16),
                        pltpu.VMEM((d // LANES, tm, LANES), F32)],
        compiler_params=_params("parallel", "parallel", "arbitrary"),
        name="attn_inproj",
    )(x, mod, gain, w, w, w)


def _attn_kernel(q0, k0p, k0, v0p, v0, q1, k1p, k1, v1p, v1, q2, k2p, k2, v2p, v2,
                 o_ref, pv0, m0, l0, pv1, m1, l1, pv2, m2, l2):
    blk = DSW_BLOCK
    tile = pl.program_id(1)
    row = lax.broadcasted_iota(jnp.int32, (blk, 2 * blk), 0)
    col = lax.broadcasted_iota(jnp.int32, (blk, 2 * blk), 1)
    band = (col >= row) & (col <= row + blk)
    bias_mid = jnp.where(band, 0.0, -jnp.inf).astype(F32)
    lo = jnp.where(tile == 0, blk, 0)
    bias_head = jnp.where(band & (col >= lo), 0.0, -jnp.inf).astype(F32)
    lane = lax.broadcasted_iota(jnp.int32, (blk, LANES), 1)
    left = lane < DSW_HEAD_DIM

    def band_block(q, kk, vv, bias):
        res = []
        for side in range(2):
            keep = left if side == 0 else jnp.logical_not(left)
            qm = jnp.where(keep, q, jnp.zeros_like(q))
            s = _dot_nt(qm, kk) + bias
            m = jnp.max(s, axis=-1, keepdims=True)
            p = jnp.exp2(s - m)
            l = jnp.sum(p, axis=-1, keepdims=True)
            res.append((_dot(p.astype(BF16), vv), m, l))
        return tuple(jnp.where(left, a, b) for a, b in zip(res[0], res[1]))

    def emit(refs, rows, q, kk, vv, bias):
        pv, m, l = band_block(q, kk, vv, bias)
        refs[0][rows, :] = pv
        refs[1][rows, :] = m
        refs[2][rows, :] = l

    cat = lambda parts: jnp.concatenate(parts, axis=0)

    g0 = (pv0, m0, l0)

    def emit0(nb):
        aligned = (lambda v: v) if isinstance(nb, int) else (lambda v: pl.multiple_of(v, blk))
        rows = pl.ds(aligned(nb * blk), blk)
        krows = pl.ds(aligned((nb - 1) * blk), 2 * blk)
        emit(g0, rows, q0[rows, :], k0[krows, :], v0[krows, :], bias_mid)

    emit(g0, pl.ds(0, blk), q0[0:blk, :], cat([k0p[...], k0[0:blk, :]]),
         cat([v0p[...], v0[0:blk, :]]), bias_head)
    for nb in range(1, G0_UNROLL + 1):
        emit0(nb)

    def body0(i, carry):
        for t in range(G0_UNROLL):
            emit0(1 + (i + 1) * G0_UNROLL + t)
        return carry

    lax.fori_loop(0, (DSW_TILE // blk - 1) // G0_UNROLL - 1, body0, 0)

    g1 = (pv1, m1, l1)
    d1 = DSW_DILATIONS[1]

    def body1(i, carry):
        for t in range(G1_UNROLL):
            residue_blocks(i * G1_UNROLL + t)
        return carry

    def residue_blocks(r):
        for nb in range(4):
            u, hb = divmod(nb, 2)
            q = q1[u, r, hb * blk:(hb + 1) * blk, :]
            if nb == 0:
                kk = cat([k1p[r], k1[0, r, 0:blk, :]])
                vv = cat([v1p[r], v1[0, r, 0:blk, :]])
            elif nb == 2:
                kk = cat([k1[0, r, blk:2 * blk, :], k1[1, r, 0:blk, :]])
                vv = cat([v1[0, r, blk:2 * blk, :], v1[1, r, 0:blk, :]])
            else:
                kk = k1[u, r, :, :]
                vv = v1[u, r, :, :]
            rows = pl.ds(nb * blk * d1 + r, blk, stride=d1)
            emit(g1, rows, q, kk, vv, bias_head if nb == 0 else bias_mid)

    lax.fori_loop(0, d1 // G1_UNROLL, body1, 0)

    g2 = (pv2, m2, l2)
    d2 = DSW_DILATIONS[2]

    def body2(i, carry):
        for t in range(G2_UNROLL):
            r = i * G2_UNROLL + t
            q = cat([q2[0, r], q2[1, r]])
            kk = cat([k2p[0, r], k2p[1, r], k2[0, r], k2[1, r]])
            vv = cat([v2p[0, r], v2p[1, r], v2[0, r], v2[1, r]])
            emit(g2, pl.ds(r, blk, stride=d2), q, kk, vv, bias_head)
        return carry

    lax.fori_loop(0, d2 // G2_UNROLL, body2, 0)

    def merge(i, carry):
        rows = pl.ds(pl.multiple_of(i * blk, blk), blk)
        ma, mb, mc = m0[rows, :], m1[rows, :], m2[rows, :]
        mx = jnp.maximum(jnp.maximum(ma, mb), mc)
        ea, eb, ec = jnp.exp2(ma - mx), jnp.exp2(mb - mx), jnp.exp2(mc - mx)
        num = pv0[rows, :] * ea + pv1[rows, :] * eb + pv2[rows, :] * ec
        den = l0[rows, :] * ea + l1[rows, :] * eb + l2[rows, :] * ec
        o_ref[rows, :] = (num / den).astype(o_ref.dtype)
        return carry

    lax.fori_loop(0, DSW_TILE // blk, merge, 0)


def _attention(qkv):
    batch, n_slots, seq, lanes = qkv.shape
    blk, unit, tile = DSW_BLOCK, DSW_UNIT, DSW_TILE
    n_tiles = seq // tile
    upt = tile // unit
    d1, d2 = DSW_DILATIONS[1], DSW_DILATIONS[2]

    def slot(g, t, j):
        return (g * 3 + t) * HEAD_PAIRS + j

    flat_blocks = qkv.reshape(batch, n_slots, seq // blk, blk, lanes)
    v1_cur = qkv.reshape(batch, n_slots, n_tiles, upt, d1, unit // d1, lanes)
    v1_prev = qkv.reshape(batch, n_slots, seq // unit, d1, unit // d1 // blk, blk, lanes)
    v2_cur = qkv.reshape(batch, n_slots, n_tiles, upt, d2, unit // d2, lanes)

    def specs(g, t):
        if g == 0:
            cur = pl.BlockSpec((None, None, tile, lanes), lambda b, i, j: (b, slot(0, t, j), i, 0))
            prev = pl.BlockSpec(
                (None, None, None, blk, lanes),
                lambda b, i, j: (b, slot(0, t, j), jnp.maximum(i * (tile // blk) - 1, 0), 0, 0))
            return (prev, flat_blocks), (cur, qkv)
        if g == 1:
            cur = pl.BlockSpec((None, None, None, upt, d1, unit // d1, lanes),
                               lambda b, i, j: (b, slot(1, t, j), i, 0, 0, 0, 0))
            prev = pl.BlockSpec(
                (None, None, None, d1, None, blk, lanes),
                lambda b, i, j: (b, slot(1, t, j), jnp.maximum(i * upt - 1, 0), 0,
                                 unit // d1 // blk - 1, 0, 0))
            return (prev, v1_prev), (cur, v1_cur)
        cur = pl.BlockSpec((None, None, None, upt, d2, unit // d2, lanes),
                           lambda b, i, j: (b, slot(2, t, j), i, 0, 0, 0, 0))
        prev = pl.BlockSpec((None, None, None, upt, d2, unit // d2, lanes),
                            lambda b, i, j: (b, slot(2, t, j), jnp.maximum(i - 1, 0), 0, 0, 0, 0))
        return (prev, v2_cur), (cur, v2_cur)

    in_specs, args = [], []
    for g in range(DSW_GROUPS):
        for t in range(3):
            prev, cur = specs(g, t)
            for spec, arr in ((prev, cur) if t else (cur,)):
                in_specs.append(spec)
                args.append(arr)
    return pl.pallas_call(
        _attn_kernel,
        grid=(batch, n_tiles, HEAD_PAIRS),
        in_specs=in_specs,
        out_specs=pl.BlockSpec((None, tile, lanes), lambda b, i, j: (b, i, j)),
        out_shape=jax.ShapeDtypeStruct((batch, seq, HEAD_PAIRS * lanes), BF16),
        scratch_shapes=[pltpu.VMEM((tile, lanes), F32)] * (3 * DSW_GROUPS),
        compiler_params=_params("parallel", "parallel", "parallel"),
        name="dilated_attention",
    )(*args)


def _ffn_kernel(*refs, final):
    if final:
        (o_ref, wo_ref, x_ref, mod_ref, gain_ref, wup_ref, cw_ref, cb_ref, wdn_ref, fg_ref,
         out_ref, carry_ref, act_ref) = refs
    else:
        (o_ref, wo_ref, x_ref, mod_ref, gain_ref, wup_ref, cw_ref, cb_ref, wdn_ref,
         out_ref, carry_ref, act_ref) = refs
    tm = x_ref.shape[0]
    tf = FFN_CHUNK

    @pl.when(pl.program_id(1) == 0)
    def _():
        carry_ref[...] = jnp.zeros_like(carry_ref)

    x = x_ref[...] + mod_ref[2:3, :] * _dot(o_ref[...], wo_ref[...])
    h = _modulate(x, gain_ref[...], mod_ref[3:4, :], mod_ref[4:5, :]).astype(BF16)
    row = lax.broadcasted_iota(jnp.int32, (tm, tf), 0)
    for c in range(FFN_DIM // tf):
        cs = slice(c * tf, (c + 1) * tf)
        us = slice(FFN_DIM + c * tf, FFN_DIM + (c + 1) * tf)
        a = _dot(h, wup_ref[:, cs])
        u = _dot(h, wup_ref[:, us])
        prev = carry_ref[:, cs]
        a1 = jnp.where(row == 0, prev[CONV_HALO - 1:CONV_HALO, :], pltpu.roll(a, 1, axis=0))
        a2 = jnp.where(row == 0, prev[CONV_HALO - 2:CONV_HALO - 1, :],
                       jnp.where(row == 1, prev[CONV_HALO - 1:CONV_HALO, :],
                                 pltpu.roll(a, 2, axis=0)))
        carry_ref[:, cs] = a[tm - CONV_HALO:tm, :]
        ac = cw_ref[2:3, cs] * a + cw_ref[1:2, cs] * a1 + cw_ref[0:1, cs] * a2 + cb_ref[:, cs]
        act_ref[:, cs] = (_silu(ac) * u).astype(BF16)
    y = x + mod_ref[5:6, :] * _dot(act_ref[...], wdn_ref[...])
    if final:
        y = _rms(y) * fg_ref[...]
    out_ref[...] = y


def _ffn(o, w_out, x, mod, gain, w_up, conv_w, conv_b, w_down, final_gain=None, tm=512):
    batch, seq, d = x.shape
    f = FFN_DIM
    final = final_gain is not None
    const2 = lambda b, i: (0, 0)
    resident = dict(pipeline_mode=pl.Buffered(1))
    in_specs = [
        pl.BlockSpec((None, tm, d), lambda b, i: (b, i, 0)),
        pl.BlockSpec((d, d), const2, **resident),
        pl.BlockSpec((None, tm, d), lambda b, i: (b, i, 0)),
        pl.BlockSpec((None, 6, d), lambda b, i: (b, 0, 0)),
        pl.BlockSpec((1, d), const2),
        pl.BlockSpec((d, 2 * f), const2, **resident),
        pl.BlockSpec((3, f), const2),
        pl.BlockSpec((1, f), const2),
        pl.BlockSpec((f, d), const2, **resident),
    ]
    args = [o, w_out, x, mod, gain, w_up, conv_w, conv_b, w_down]
    if final:
        in_specs.append(pl.BlockSpec((1, d), const2))
        args.append(final_gain)
    return pl.pallas_call(
        functools.partial(_ffn_kernel, final=final),
        grid=(batch, seq // tm),
        in_specs=in_specs,
        out_specs=pl.BlockSpec((None, tm, d), lambda b, i: (b, i, 0)),
        out_shape=jax.ShapeDtypeStruct((batch, seq, d), F32),
        scratch_shapes=[pltpu.VMEM((CONV_HALO, f), F32), pltpu.VMEM((tm, f), BF16)],
        compiler_params=_params("arbitrary", "arbitrary"),
        name="conv_ffn",
    )(*args)


def _gla_inproj_kernel(x_ref, mod_ref, gain_ref, w_ref, wl_ref, wg_ref, bg_ref,
                       q_ref, k_ref, v_ref, r_ref, la_ref):
    h = _modulate(x_ref[...], gain_ref[...], mod_ref[0:1, :], mod_ref[1:2, :]).astype(BF16)
    q_ref[...] = _dot(h, w_ref[:, 0:GLA_DK])
    k_ref[...] = _dot(h, w_ref[:, GLA_DK:2 * GLA_DK])
    v_ref[...] = _dot(h, w_ref[:, 2 * GLA_DK:2 * GLA_DK + GLA_DV]).astype(BF16)
    r_ref[...] = _dot(h, w_ref[:, 2 * GLA_DK + GLA_DV:])
    g_low = _dot(h, wl_ref[...])
    z = _dot(g_low.astype(BF16), wg_ref[...]) + bg_ref[...]
    log_sig = jnp.minimum(z, 0.0) - jnp.log(1.0 + jnp.exp(-jnp.abs(z)))
    la_ref[...] = log_sig / GLA_TAU


def _gla_inproj(x, mod, gain, w_main, w_low, w_gate, b_gate, tm=512):
    batch, seq, d = x.shape
    const2 = lambda b, i: (0, 0)
    row_spec = lambda n: pl.BlockSpec((None, tm, n), lambda b, i: (b, i, 0))
    shape = lambda n, dt: jax.ShapeDtypeStruct((batch, seq, n), dt)
    return pl.pallas_call(
        _gla_inproj_kernel,
        grid=(batch, seq // tm),
        in_specs=[
            row_spec(d),
            pl.BlockSpec((None, 6, d), lambda b, i: (b, 0, 0)),
            pl.BlockSpec((1, d), const2),
            pl.BlockSpec(w_main.shape, const2),
            pl.BlockSpec(w_low.shape, const2),
            pl.BlockSpec(w_gate.shape, const2),
            pl.BlockSpec((1, GLA_DK), const2),
        ],
        out_specs=[row_spec(GLA_DK), row_spec(GLA_DK), row_spec(GLA_DV), row_spec(GLA_DV),
                   row_spec(GLA_DK)],
        out_shape=[shape(GLA_DK, F32), shape(GLA_DK, F32), shape(GLA_DV, BF16),
                   shape(GLA_DV, F32), shape(GLA_DK, F32)],
        compiler_params=_params("parallel", "parallel"),
        name="gla_inproj",
    )(x, mod, gain, w_main, w_low, w_gate, b_gate)


def _gla_kernel(q_ref, k_ref, v_ref, r_ref, la_ref, gain_ref, tri_ref, o_ref, st_ref):
    @pl.when(pl.program_id(1) == 0)
    def _():
        st_ref[...] = jnp.zeros_like(st_ref)

    ck = GLA_CHUNK
    la = la_ref[...]
    la_hi = la.astype(BF16)
    la_lo = (la - la_hi.astype(F32)).astype(BF16)
    tri = tri_ref[...]
    b = _dot(tri, la_hi) + _dot(tri, la_lo)
    ri = lax.broadcasted_iota(jnp.int32, (ck, ck), 0)
    ci = lax.broadcasted_iota(jnp.int32, (ck, ck), 1)
    causal = ri >= ci
    gain = gain_ref[...]
    scale = GLA_HK ** -0.5

    for c in range(GLA_TILE // ck):
        rs = slice(c * ck, (c + 1) * ck)
        bc = b[rs, :]
        bl = bc[ck - 1:ck, :]
        kc = k_ref[rs, :]
        qd = (q_ref[rs, :] * scale * jnp.exp(bc)).astype(BF16)
        kd = (kc * jnp.exp(-bc)).astype(BF16)
        ke = (kc * jnp.exp(bl - bc)).astype(BF16)
        dec = jnp.exp(bl)
        for hd in range(GLA_HEADS):
            ks = slice(hd * GLA_HK, (hd + 1) * GLA_HK)
            vs = slice(hd * GLA_HV, (hd + 1) * GLA_HV)
            vc = v_ref[rs, vs]
            att = jnp.where(causal, _dot_nt(qd[:, ks], kd[:, ks]), 0.0)
            st = st_ref[hd]
            o = _dot(att.astype(BF16), vc) + _dot_nt(qd[:, ks], st.astype(BF16))
            st_ref[hd] = st * dec[:, ks] + _dot_tn(vc, ke[:, ks])
            rr = r_ref[rs, vs]
            o_ref[rs, vs] = ((_rms(o) * gain) * _silu(rr)).astype(o_ref.dtype)


def _gla(q, k, v, r, log_a, gain):
    batch, seq, _ = q.shape
    t = GLA_TILE
    idx = jnp.arange(t)
    tri = ((idx[:, None] >= idx[None, :]) &
           (idx[:, None] // GLA_CHUNK == idx[None, :] // GLA_CHUNK)).astype(BF16)
    row_spec = lambda n: pl.BlockSpec((None, t, n), lambda b, i: (b, i, 0))
    const2 = lambda b, i: (0, 0)
    return pl.pallas_call(
        _gla_kernel,
        grid=(batch, seq // t),
        in_specs=[row_spec(GLA_DK), row_spec(GLA_DK), row_spec(GLA_DV), row_spec(GLA_DV),
                  row_spec(GLA_DK), pl.BlockSpec((1, GLA_HV), const2), pl.BlockSpec((t, t), const2)],
        out_specs=row_spec(GLA_DV),
        out_shape=jax.ShapeDtypeStruct((batch, seq, GLA_DV), BF16),
        scratch_shapes=[pltpu.VMEM((GLA_HEADS, GLA_HV, GLA_HK), F32)],
        compiler_params=_params("arbitrary", "arbitrary"),
        name="gla_recurrence",
    )(q, k, v, r, log_a, gain, tri)


def kernel(x, c, w_ada, b_ada, norm_mix, norm_ffn, attn_w_in, attn_w_out,
           gla_w_in, gla_w_gate, gla_b_gate, gla_norm, gla_w_out,
           ffn_w_up, ffn_conv_w, ffn_conv_b, ffn_w_down, norm_final):
    batch, seq, d = x.shape
    depth = w_ada.shape[0]
    mods = _ada(c, w_ada, b_ada).reshape(depth, batch, 6, d)

    mod = mods[0]
    qkv = _attn_inproj(x, mod, norm_mix[0].reshape(1, d), attn_w_in[0].astype(BF16))
    o = _attention(qkv)
    x = _ffn(o, attn_w_out[0].astype(BF16), x, mod, norm_ffn[0].reshape(1, d),
             ffn_w_up[0].astype(BF16), ffn_conv_w[0], ffn_conv_b[0].reshape(1, FFN_DIM),
             ffn_w_down[0].astype(BF16))

    mod = mods[1]
    w_in = gla_w_in[0]
    n_main = 2 * GLA_DK + GLA_DV
    w_main = jnp.concatenate([w_in[:, :n_main], w_in[:, n_main + GLA_RANK:]], axis=1).astype(BF16)
    w_low = w_in[:, n_main:n_main + GLA_RANK].astype(BF16)
    q, k, v, r, log_a = _gla_inproj(x, mod, norm_mix[1].reshape(1, d), w_main, w_low,
                                    gla_w_gate[0].astype(BF16), gla_b_gate[0].reshape(1, GLA_DK))
    o = _gla(q, k, v, r, log_a, gla_norm[0].reshape(1, GLA_HV))
    x = _ffn(o, gla_w_out[0].astype(BF16), x, mod, norm_ffn[1].reshape(1, d),
             ffn_w_up[1].astype(BF16), ffn_conv_w[1], ffn_conv_b[1].reshape(1, FFN_DIM),
             ffn_w_down[1].astype(BF16), final_gain=norm_final.reshape(1, d))
    return x
```
